```python
import math
import jax, jax.numpy as jnp
from jax import lax
import numpy as np

D_MODEL = 1024
BATCH = 4
SEQ = 4096
DEPTH = 2

BRANCH_WIDTH = D_MODEL // 2
N_BRANCH = 3
HG_HEADS = 4
HG_HEAD_DIM = BRANCH_WIDTH // HG_HEADS
HG_CHUNK = 16
S5_GROUP_CH = 16
S5_GROUPS = BRANCH_WIDTH // S5_GROUP_CH
S5_STATE = 64
RG_BLOCKS = 4
RG_BLOCK_W = BRANCH_WIDTH // RG_BLOCKS
RG_CONV = 4
RG_C = 8.0
IN_COLS = 4 * BRANCH_WIDTH + BRANCH_WIDTH + 2 * BRANCH_WIDTH + N_BRANCH * D_MODEL
N_EXPERTS = 64
N_EXPERT_GROUPS = 8
TOPK_GROUPS = 4
TOP_K = 8
EXPERT_HIDDEN = D_MODEL // 4
ROUTED_SCALE = 2.5
MOE_BLOCK = 128
DN_ALPHA = (2 * DEPTH) ** 0.25
DN_BETA = (8 * DEPTH) ** -0.25
LN_EPS = 1e-5
RMS_EPS = 1e-6

kernel_name = 'hybrid_hgrn2_s5_rglru_moe_deepnorm'


def _layernorm(x, g, b):
    xf = x.astype(jnp.float32)
    mu = jnp.mean(xf, axis=-1, keepdims=True)
    var = jnp.mean(jnp.square(xf - mu), axis=-1, keepdims=True)
    y = (xf - mu) * lax.rsqrt(var + LN_EPS) * g.astype(jnp.float32) + b.astype(jnp.float32)
    return y.astype(x.dtype)


def _lin_combine(left, right):
    a_l, b_l = left
    a_r, b_r = right
    return a_l * a_r, a_r * b_l + b_r


def _hgrn2(q, f_pre, v, g, lb, norm_w):
    Bn, S, _ = q.shape
    nc = S // HG_CHUNK
    f32 = jnp.float32

    def heads(t):
        return t.reshape(Bn, nc, HG_CHUNK, HG_HEADS, HG_HEAD_DIM).transpose(0, 3, 1, 2, 4)

    lbf = lb.astype(f32)
    f = lbf + (1.0 - lbf) * jax.nn.sigmoid(f_pre.astype(f32))
    qh = heads(jax.nn.silu(q.astype(f32)))
    kh = heads(1.0 - f)
    vh = heads(v.astype(f32))
    bcum = jnp.cumsum(heads(jnp.log(f)), axis=3)
    causal = jnp.tril(jnp.ones((HG_CHUNK, HG_CHUNK), dtype=bool))
    diff = bcum[:, :, :, :, None, :] - bcum[:, :, :, None, :, :]
    decay = jnp.exp(jnp.where(causal[:, :, None], diff, -jnp.inf))
    scores = jnp.sum(qh[:, :, :, :, None, :] * kh[:, :, :, None, :, :] * decay, axis=-1)
    o_intra = jnp.einsum('bhnts,bhnsv->bhntv', scores, vh)
    b_last = bcum[:, :, :, -1, :]
    kv = jnp.einsum('bhnsd,bhnsv->bhndv', kh * jnp.exp(b_last[:, :, :, None, :] - bcum), vh)

    def step(state, inp):
        dec, upd = inp
        return dec[..., None] * state + upd, state

    init = jnp.zeros((Bn, HG_HEADS, HG_HEAD_DIM, HG_HEAD_DIM), f32)
    _, s_prev = lax.scan(step, init, (jnp.moveaxis(jnp.exp(b_last), 2, 0), jnp.moveaxis(kv, 2, 0)))
    s_prev = jnp.moveaxis(s_prev, 0, 2)
    o = o_intra + jnp.einsum('bhntd,bhndv->bhntv', qh * jnp.exp(bcum), s_prev)
    o = o.transpose(0, 2, 3, 1, 4).reshape(Bn, S, HG_HEADS, HG_HEAD_DIM)
    o = o * lax.rsqrt(jnp.mean(jnp.square(o), axis=-1, keepdims=True) + RMS_EPS)
    o = o.reshape(Bn, S, BRANCH_WIDTH) * norm_w.astype(f32) * jax.nn.silu(g.astype(f32))
    return o.astype(q.dtype)


def _s5(u, a_re, a_im, b_re, b_im, c_re, c_im, d, log_dt, glu_w, glu_b):
    Bn, S, _ = u.shape
    f32 = jnp.float32
    uf = u.astype(f32).reshape(Bn, S, S5_GROUPS, S5_GROUP_CH)
    A = lax.complex(a_re.astype(f32), a_im.astype(f32))
    dt = jnp.exp(log_dt.astype(f32))[:, None]
    A_bar = jnp.exp(A * dt)
    B_bar = ((A_bar - 1.0) / A)[..., None] * lax.complex(b_re.astype(f32), b_im.astype(f32))
    Cm = lax.complex(c_re.astype(f32), c_im.astype(f32))
    bu = jnp.einsum('bsgc,gpc->bsgp', uf.astype(jnp.complex64), B_bar)
    a = jnp.broadcast_to(A_bar[None, None], (1, S, S5_GROUPS, S5_STATE))
    _, h = lax.associative_scan(_lin_combine, (a, bu), axis=1)
    y = jnp.einsum('bsgp,gcp->bsgc', h, Cm).real + d.astype(f32).reshape(S5_GROUPS, S5_GROUP_CH) * uf
    y = jax.nn.gelu(y.reshape(Bn, S, BRANCH_WIDTH))
    y = y * jax.nn.sigmoid(jnp.einsum('bsc,ce->bse', y, glu_w.astype(f32)) + glu_b.astype(f32))
    return y.astype(u.dtype)


def _rglru(xg, xr, conv_w, conv_b, wa, ba, wx, bx, lam):
    Bn, S, _ = xr.shape
    f32 = jnp.float32
    xc = lax.conv_general_dilated(
        xr.astype(f32), conv_w.astype(f32)[:, None, :], window_strides=(1,),
        padding=[(RG_CONV - 1, 0)], dimension_numbers=('NWC', 'WIO', 'NWC'),
        feature_group_count=BRANCH_WIDTH) + conv_b.astype(f32)
    xb = xc.reshape(Bn, S, RG_BLOCKS, RG_BLOCK_W)
    r = jax.nn.sigmoid(jnp.einsum('bshi,hij->bshj', xb, wa.astype(f32)).reshape(Bn, S, BRANCH_WIDTH) + ba.astype(f32))
    i = jax.nn.sigmoid(jnp.einsum('bshi,hij->bshj', xb, wx.astype(f32)).reshape(Bn, S, BRANCH_WIDTH) + bx.astype(f32))
    log_a = -RG_C * r * jax.nn.softplus(-lam.astype(f32))
    a = jnp.exp(log_a)
    b = jnp.sqrt(-jnp.expm1(2.0 * log_a)) * (i * xc)
    _, h = lax.associative_scan(_lin_combine, (a, b), axis=1)
    return (h * jax.nn.gelu(xg.astype(f32))).astype(xr.dtype)


def _swiglu(x, w1, w3, w2):
    return (jax.nn.silu(x @ w1) * (x @ w3)) @ w2


def _moe(x2d, router_w, router_bias, w1, w3, w2, sw1, sw3, sw2):
    N, D = x2d.shape
    f32 = jnp.float32
    scores = jax.nn.sigmoid(x2d.astype(f32) @ router_w.astype(f32))
    biased = scores + router_bias.astype(f32)
    per_group = N_EXPERTS // N_EXPERT_GROUPS
    grp_score = jnp.sum(lax.top_k(biased.reshape(N, N_EXPERT_GROUPS, per_group), 2)[0], axis=-1)
    _, grp_idx = lax.top_k(grp_score, TOPK_GROUPS)
    grp_mask = jnp.any(grp_idx[:, :, None] == jnp.arange(N_EXPERT_GROUPS)[None, None, :], axis=1)
    masked = jnp.where(jnp.repeat(grp_mask, per_group, axis=1), biased, -jnp.inf)
    _, top_idx = lax.top_k(masked, TOP_K)
    w = jnp.take_along_axis(scores, top_idx, axis=1)
    w = w / jnp.sum(w, axis=-1, keepdims=True) * ROUTED_SCALE

    nk = N * TOP_K
    e_flat = top_idx.reshape(-1)
    tok_flat = jnp.arange(nk, dtype=jnp.int32) // TOP_K
    w_flat = w.reshape(-1)
    order = jnp.argsort(e_flat)
    e_sorted = e_flat[order]
    counts = jnp.zeros((N_EXPERTS,), jnp.int32).at[e_flat].add(1)
    start = jnp.cumsum(counts) - counts
    padded = (counts + MOE_BLOCK - 1) // MOE_BLOCK * MOE_BLOCK
    pad_end = jnp.cumsum(padded)
    pad_start = pad_end - padded
    dest = pad_start[e_sorted] + (jnp.arange(nk, dtype=jnp.int32) - start[e_sorted])
    n_blocks = (nk + N_EXPERTS * (MOE_BLOCK - 1) + MOE_BLOCK - 1) // MOE_BLOCK
    n_rows = n_blocks * MOE_BLOCK
    row_tok = jnp.full((n_rows,), N, jnp.int32).at[dest].set(tok_flat[order])
    row_w = jnp.zeros((n_rows,), f32).at[dest].set(w_flat[order])
    block_start = jnp.arange(n_blocks, dtype=jnp.int32) * MOE_BLOCK
    block_exp = jnp.minimum(jnp.searchsorted(pad_end, block_start, side='right'), N_EXPERTS - 1)
    x_pad = jnp.concatenate([x2d, jnp.zeros((1, D), x2d.dtype)], axis=0)

    def block_fn(acc, inp):
        rows, wts, e = inp
        xb = x_pad[rows]
        yb = _swiglu(xb, w1[e], w3[e], w2[e]) * wts[:, None].astype(x2d.dtype)
        return acc.at[rows].add(yb), None

    acc0 = jnp.zeros((N + 1, D), x2d.dtype)
    routed, _ = lax.scan(block_fn, acc0, (row_tok.reshape(n_blocks, MOE_BLOCK),
                                          row_w.reshape(n_blocks, MOE_BLOCK), block_exp))
    return routed[:N] + _swiglu(x2d, sw1, sw3, sw2)


def setup_inputs(seed: int = 0) -> dict:
    key = jax.random.key(seed)
    ks = jax.random.split(key, 36)
    L, D, W = DEPTH, D_MODEL, BRANCH_WIDTH
    G, P, CH = S5_GROUPS, S5_STATE, S5_GROUP_CH
    E, H = N_EXPERTS, EXPERT_HIDDEN

    def nrm(i, shape, s):
        return s * jax.random.normal(ks[i], shape, jnp.float32)

    def uni(i, shape, lo, hi):
        return jax.random.uniform(ks[i], shape, jnp.float32, lo, hi)

    a0 = uni(21, (L, W), 0.9, 0.999)
    s = a0 ** (1.0 / RG_C)
    return {
        'x': nrm(0, (BATCH, SEQ, D), 1.0),
        'w_in': nrm(1, (L, D, IN_COLS), D ** -0.5),
        'b_gate': nrm(2, (L, N_BRANCH * D), 0.02),
        'hgrn_lb_logits': nrm(3, (L, W), 0.5),
        'hgrn_norm_w': 1.0 + nrm(4, (L, W), 0.02),
        's5_a_re': -0.5 + nrm(5, (L, G, P), 0.01),
        's5_a_im': math.pi * jnp.arange(P, dtype=jnp.float32) + nrm(6, (L, G, P), 0.01),
        's5_b_re': nrm(7, (L, G, P, CH), (2 * CH) ** -0.5),
        's5_b_im': nrm(8, (L, G, P, CH), (2 * CH) ** -0.5),
        's5_c_re': nrm(9, (L, G, CH, P), (2 * P) ** -0.5),
        's5_c_im': nrm(10, (L, G, CH, P), (2 * P) ** -0.5),
        's5_d': nrm(11, (L, W), 0.5),
        's5_log_dt': uni(12, (L, G), math.log(1e-3), math.log(1e-1)),
        's5_glu_w': nrm(13, (L, W, W), W ** -0.5),
        's5_glu_b': nrm(14, (L, W), 0.02),
        'rg_conv_w': nrm(15, (L, RG_CONV, W), RG_CONV ** -0.5),
        'rg_conv_b': nrm(16, (L, W), 0.02),
        'rg_wa': nrm(17, (L, RG_BLOCKS, RG_BLOCK_W, RG_BLOCK_W), RG_BLOCK_W ** -0.5),
        'rg_ba': nrm(18, (L, W), 0.02),
        'rg_wx': nrm(19, (L, RG_BLOCKS, RG_BLOCK_W, RG_BLOCK_W), RG_BLOCK_W ** -0.5),
        'rg_bx': nrm(20, (L, W), 0.02),
        'rg_lambda': jnp.log(s) - jnp.log1p(-s),
        'w_branch': nrm(22, (L, N_BRANCH, W, D), DN_BETA * W ** -0.5),
        'w_out': nrm(23, (L, D, D), DN_BETA * D ** -0.5),
        'ln1_g': 1.0 + nrm(24, (L, D), 0.02),
        'ln1_b': nrm(25, (L, D), 0.02),
        'router_w': nrm(26, (L, D, E), D ** -0.5),
        'router_bias': nrm(27, (L, E), 0.01),
        'exp_w1': nrm(28, (L, E, D, H), D ** -0.5),
        'exp_w3': nrm(29, (L, E, D, H), D ** -0.5),
        'exp_w2': nrm(30, (L, E, H, D), DN_BETA * H ** -0.5),
        'sh_w1': nrm(31, (L, D, H), D ** -0.5),
        'sh_w3': nrm(32, (L, D, H), D ** -0.5),
        'sh_w2': nrm(33, (L, H, D), DN_BETA * H ** -0.5),
        'ln2_g': 1.0 + nrm(34, (L, D), 0.02),
        'ln2_b': nrm(35, (L, D), 0.02),
    }


def reference(x, w_in, b_gate, hgrn_lb_logits, hgrn_norm_w, s5_a_re, s5_a_im, s5_b_re, s5_b_im,
              s5_c_re, s5_c_im, s5_d, s5_log_dt, s5_glu_w, s5_glu_b, rg_conv_w, rg_conv_b,
              rg_wa, rg_ba, rg_wx, rg_bx, rg_lambda, w_branch, w_out, ln1_g, ln1_b,
              router_w, router_bias, exp_w1, exp_w3, exp_w2, sh_w1, sh_w3, sh_w2, ln2_g, ln2_b):
    Bn, S, D = x.shape
    sm = jax.nn.softmax(hgrn_lb_logits.astype(jnp.float32), axis=0)
    lower_bounds = jnp.cumsum(sm, axis=0) - sm[0:1]
    splits = np.cumsum([BRANCH_WIDTH] * 7).tolist()
    for l in range(DEPTH):
        proj = jnp.einsum('bsd,dc->bsc', x, w_in[l])
        hq, hf, hi, hg, su, rg_g, rg_x, gate_pre = jnp.split(proj, splits, axis=-1)
        y_a = _hgrn2(hq, hf, hi, hg, lower_bounds[l], hgrn_norm_w[l])
        y_b = _s5(su, s5_a_re[l], s5_a_im[l], s5_b_re[l], s5_b_im[l], s5_c_re[l], s5_c_im[l],
                  s5_d[l], s5_log_dt[l], s5_glu_w[l], s5_glu_b[l])
        y_c = _rglru(rg_g, rg_x, rg_conv_w[l], rg_conv_b[l], rg_wa[l], rg_ba[l], rg_wx[l], rg_bx[l],
                     rg_lambda[l])
        ys = jnp.stack([y_a, y_b, y_c], axis=2)
        branch = jnp.einsum('bskc,kcd->bskd', ys, w_branch[l])
        gates = jax.nn.sigmoid(gate_pre.reshape(Bn, S, N_BRANCH, D) + b_gate[l].reshape(N_BRANCH, D))
        mix = jnp.einsum('bsd,de->bse', jnp.sum(gates * branch, axis=2), w_out[l])
        x = _layernorm(DN_ALPHA * x + mix, ln1_g[l], ln1_b[l])
        ffn = _moe(x.reshape(Bn * S, D), router_w[l], router_bias[l], exp_w1[l], exp_w3[l], exp_w2[l],
                   sh_w1[l], sh_w3[l], sh_w2[l]).reshape(Bn, S, D)
        x = _layernorm(DN_ALPHA * x + ffn, ln2_g[l], ln2_b[l])
    return x
```

```python
import functools
import math

import numpy as np
import jax
import jax.numpy as jnp
from jax import lax
from jax.experimental import pallas as pl
from jax.experimental.pallas import tpu as pltpu

F32 = jnp.float32
BF16 = jnp.bfloat16

HG_HEADS = 4
HG_HEAD_DIM = 128
HG_CHUNK = 16
S5_GROUP_CH = 16
S5_STATE = 64
RG_BLOCK_W = 128
RG_CONV = 4
RG_C = 8.0
N_EXPERTS = 64
N_EXPERT_GROUPS = 8
TOPK_GROUPS = 4
TOP_K = 8
ROUTED_SCALE = 2.5
LN_EPS = 1e-5
RMS_EPS = 1e-6
LANES = 128
SUBLANES = 8
VMEM_LIMIT = 56 * 1024 * 1024


def _cparams(sem):
    return pltpu.CompilerParams(dimension_semantics=sem, vmem_limit_bytes=VMEM_LIMIT)


def _sigmoid(x):
    return 1.0 / (1.0 + jnp.exp(-x))


def _silu(x):
    return x * _sigmoid(x)


def _gelu_tanh(x):
    c = math.sqrt(2.0 / math.pi)
    return 0.5 * x * (1.0 + jnp.tanh(c * (x + 0.044715 * (x * x * x))))


def _layernorm(z, g, b):
    mu = jnp.mean(z, axis=-1, keepdims=True)
    zc = z - mu
    var = jnp.mean(zc * zc, axis=-1, keepdims=True)
    return zc * lax.rsqrt(var + LN_EPS) * g + b


def _dot(a, b):
    return jnp.dot(a, b, preferred_element_type=F32)


def _matmul_kernel(x_ref, w_ref, o_ref, xb_ref):
    @pl.when(pl.program_id(1) == 0)
    def _():
        xb_ref[...] = x_ref[...].astype(BF16)

    o_ref[...] = _dot(xb_ref[...], w_ref[...].astype(BF16)).astype(o_ref.dtype)


def _matmul(x, w, tm, tn):
    m, k = x.shape
    n = w.shape[1]
    return pl.pallas_call(
        _matmul_kernel,
        grid=(m // tm, n // tn),
        in_specs=[pl.BlockSpec((tm, k), lambda i, j: (i, 0)),
                  pl.BlockSpec((k, tn), lambda i, j: (0, j))],
        out_specs=pl.BlockSpec((tm, tn), lambda i, j: (i, j)),
        out_shape=jax.ShapeDtypeStruct((m, n), F32),
        scratch_shapes=[pltpu.VMEM((tm, k), BF16)],
        compiler_params=_cparams(("parallel", "arbitrary")),
        name="in_proj",
    )(x, w)


def _hgrn2_kernel(q_ref, f_ref, v_ref, g_ref, lb_ref, nw_ref, o_ref, st_ref, *, batch):
    rows = q_ref.shape[0]
    cr = HG_CHUNK * batch
    n_chunks = rows // cr
    dh = HG_HEAD_DIM

    @pl.when(pl.program_id(0) == 0)
    def _():
        st_ref[...] = jnp.zeros_like(st_ref)

    row = lax.broadcasted_iota(jnp.int32, (cr, dh), 0)
    row_b = row % batch
    ones_sum = jnp.ones((dh, dh), BF16)

    def chunk(c, carry):
        r0 = pl.multiple_of(c * cr, cr)
        for h in range(HG_HEADS):
            ls = slice(h * dh, (h + 1) * dh)
            lb = lb_ref[:, ls]
            f = lb + (1.0 - lb) * _sigmoid(f_ref[pl.ds(r0, cr), ls])
            q = _silu(q_ref[pl.ds(r0, cr), ls])
            k = 1.0 - f
            v = v_ref[pl.ds(r0, cr), ls]
            bc = jnp.log(f)
            sh = batch
            while sh < cr:
                bc = bc + jnp.where(row >= sh, pltpu.roll(bc, sh, 0), 0.0)
                sh *= 2
            p0 = (q * k).astype(BF16)
            o = _dot(p0, ones_sum) * v
            for j in range(1, HG_CHUNK):
                s = j * batch
                dec = jnp.exp(bc - pltpu.roll(bc, s, 0))
                p = jnp.where(row >= s, q * pltpu.roll(k, s, 0) * dec, 0.0)
                o = o + _dot(p.astype(BF16), ones_sum) * pltpu.roll(v, s, 0)
            b_last = bc[cr - batch:cr, :]
            qt = q * jnp.exp(bc)
            kt = k * jnp.exp(jnp.concatenate([b_last] * HG_CHUNK, axis=0) - bc)
            qm = jnp.concatenate([jnp.where(row_b == b, qt, 0.0) for b in range(batch)], axis=1)
            km = jnp.concatenate([jnp.where(row_b == b, kt, 0.0) for b in range(batch)], axis=1)
            st = st_ref[h]
            o = o + lax.dot_general(qm.astype(BF16), st.astype(BF16),
                                    (((1,), (1,)), ((), ())), preferred_element_type=F32)
            kv = lax.dot_general(v.astype(BF16), km.astype(BF16),
                                 (((0,), (0,)), ((), ())), preferred_element_type=F32)
            dec_all = jnp.concatenate([jnp.exp(b_last[b:b + 1, :]) for b in range(batch)], axis=1)
            st_ref[h] = st * dec_all + kv
            o = o * lax.rsqrt(jnp.mean(o * o, axis=-1, keepdims=True) + RMS_EPS)
            o_ref[pl.ds(r0, cr), ls] = o * nw_ref[:, ls] * _silu(g_ref[pl.ds(r0, cr), ls])
        return carry

    lax.fori_loop(0, n_chunks, chunk, 0)


def _hgrn2(proj, cols, lb, norm_w, batch, rows):
    n = proj.shape[0]
    w = HG_HEADS * HG_HEAD_DIM
    cq, cf, cv, cg = cols

    def spec(cb):
        return pl.BlockSpec((rows, w), lambda i: (i, cb))

    vec = pl.BlockSpec((1, w), lambda i: (0, 0))
    return pl.pallas_call(
        functools.partial(_hgrn2_kernel, batch=batch),
        grid=(n // rows,),
        in_specs=[spec(cq), spec(cf), spec(cv), spec(cg), vec, vec],
        out_specs=pl.BlockSpec((rows, w), lambda i: (i, 0)),
        out_shape=jax.ShapeDtypeStruct((n, w), F32),
        scratch_shapes=[pltpu.VMEM((HG_HEADS, HG_HEAD_DIM, batch * HG_HEAD_DIM), F32)],
        compiler_params=_cparams(("arbitrary",)),
        name="hgrn2",
    )(proj, proj, proj, proj, lb.reshape(1, w), norm_w.reshape(1, w))


S5_TILE = 128


def _s5_kernel(u_ref, pin_ref, bm_ref, ar_ref, ai_ref, qout_ref, cm_ref, d_ref, gw_ref, gb_ref,
               o_ref, hs_ref, h_ref, *, batch):
    rows, w = u_ref.shape
    nblk = w // LANES
    nc = hs_ref.shape[1]
    cpb = nc // nblk
    tile2 = 2 * S5_TILE
    n_tiles = rows // S5_TILE
    steps = rows // batch

    @pl.when(pl.program_id(0) == 0)
    def _():
        h_ref[...] = jnp.zeros_like(h_ref)

    for t in range(n_tiles):
        ub = u_ref[t * S5_TILE:(t + 1) * S5_TILE, :].astype(BF16)
        up = _dot(pin_ref[...], ub).astype(BF16)
        for j in range(nblk):
            ls = slice(j * LANES, (j + 1) * LANES)
            lhs = jnp.concatenate([up[0:tile2, ls], up[tile2:2 * tile2, ls]], axis=1)
            hs_ref[t * tile2:(t + 1) * tile2, j * cpb:(j + 1) * cpb] = _dot(lhs, bm_ref[j])

    top = lax.broadcasted_iota(jnp.int32, (2 * batch, LANES), 0) < batch
    groups = nc // LANES
    per_pass = 8
    unroll = 4
    for p in range(groups // per_pass):
        cols = [(p * per_pass + g) * LANES for g in range(per_pass)]
        ar8 = [jnp.broadcast_to(ar_ref[:, c:c + LANES], (2 * batch, LANES)) for c in cols]
        ai8 = [jnp.where(top, -1.0, 1.0) * ai_ref[:, c:c + LANES] for c in cols]

        def body(i, hs):
            hs = list(hs)
            for s in range(unroll):
                r0 = pl.multiple_of((i * unroll + s) * (2 * batch), 2 * batch)
                for g, c in enumerate(cols):
                    h = (ar8[g] * hs[g] + ai8[g] * pltpu.roll(hs[g], batch, 0)
                         + hs_ref[pl.ds(r0, 2 * batch), c:c + LANES])
                    hs_ref[pl.ds(r0, 2 * batch), c:c + LANES] = h
                    hs[g] = h
            return tuple(hs)

        h0 = tuple(h_ref[:, c:c + LANES] for c in cols)
        hn = lax.fori_loop(0, steps // unroll, body, h0)
        for g, c in enumerate(cols):
            h_ref[:, c:c + LANES] = hn[g]

    for t in range(n_tiles):
        hb = hs_ref[t * tile2:(t + 1) * tile2, :].astype(BF16)
        hre = _dot(qout_ref[0], hb).astype(BF16)
        him = _dot(qout_ref[1], hb).astype(BF16)
        ys = []
        for j in range(nblk):
            cs = slice(j * cpb, (j + 1) * cpb)
            ys.append(_dot(jnp.concatenate([hre[:, cs], him[:, cs]], axis=1), cm_ref[j]))
        rs = slice(t * S5_TILE, (t + 1) * S5_TILE)
        y = jnp.concatenate(ys, axis=1) + d_ref[...] * u_ref[rs, :]
        y = _gelu_tanh(y)
        z = _dot(y.astype(BF16), gw_ref[...].astype(BF16)) + gb_ref[...]
        o_ref[rs, :] = y * _sigmoid(z)


def _s5_layout_matrices(batch):
    assert 2 * batch == SUBLANES, "one time step must fill the 8 sublanes"
    tile2 = 2 * S5_TILE
    r2 = np.arange(tile2)
    src = (r2 // (2 * batch)) * batch + r2 % batch
    is_re = (r2 % (2 * batch)) < batch
    onehot = (src[:, None] == np.arange(S5_TILE)[None, :])
    p_re = onehot & is_re[:, None]
    p_im = onehot & ~is_re[:, None]
    pin = np.concatenate([p_re, p_im], axis=0).astype(np.float32)
    qout = np.stack([p_re.T, p_im.T], axis=0).astype(np.float32)
    return jnp.asarray(pin, BF16), jnp.asarray(qout, BF16)


def _s5(proj, col, mats, d, glu_w, glu_b, batch, rows):
    n = proj.shape[0]
    w = d.shape[0]
    bmat, ar, ai, cmat = mats
    nblk, _, cpb = bmat.shape
    nc = nblk * cpb
    pin, qout = _s5_layout_matrices(batch)
    full = lambda a: pl.BlockSpec(a.shape, lambda i, nd=a.ndim: (0,) * nd)
    return pl.pallas_call(
        functools.partial(_s5_kernel, batch=batch),
        grid=(n // rows,),
        in_specs=[pl.BlockSpec((rows, w), lambda i: (i, col)),
                  full(pin), full(bmat), full(ar), full(ai), full(qout), full(cmat),
                  pl.BlockSpec((1, w), lambda i: (0, 0)),
                  pl.BlockSpec((w, w), lambda i: (0, 0)),
                  pl.BlockSpec((1, w), lambda i: (0, 0))],
        out_specs=pl.BlockSpec((rows, w), lambda i: (i, 0)),
        out_shape=jax.ShapeDtypeStruct((n, w), F32),
        scratch_shapes=[pltpu.VMEM((2 * rows, nc), F32),
                        pltpu.VMEM((2 * batch, nc), F32)],
        compiler_params=_cparams(("arbitrary",)),
        name="s5",
    )(proj, pin, bmat, ar, ai, qout, cmat, d.reshape(1, w), glu_w, glu_b.reshape(1, w))


def _s5_matrices(a_re, a_im, b_re, b_im, c_re, c_im, log_dt):
    g, p = a_re.shape
    ch = b_re.shape[-1]
    gpb = LANES // ch
    nblk = g // gpb
    dt = jnp.exp(log_dt.astype(F32))[:, None]
    mag = jnp.exp(a_re * dt)
    abr = mag * jnp.cos(a_im * dt)
    abi = mag * jnp.sin(a_im * dt)
    den = a_re * a_re + a_im * a_im
    cr = ((abr - 1.0) * a_re + abi * a_im) / den
    ci = (abi * a_re - (abr - 1.0) * a_im) / den
    bbr = cr[..., None] * b_re - ci[..., None] * b_im
    bbi = cr[..., None] * b_im + ci[..., None] * b_re
    eye = jnp.eye(gpb, dtype=F32)

    def in_block(m):
        m = m.reshape(nblk, gpb, p, ch)
        return jnp.einsum('jgpc,gh->jgchp', m, eye).reshape(nblk, gpb * ch, gpb * p)

    def out_block(m):
        m = m.reshape(nblk, gpb, ch, p)
        return jnp.einsum('jgcp,gh->jgphc', m, eye).reshape(nblk, gpb * p, gpb * ch)

    bmat = jnp.concatenate([in_block(bbr), in_block(bbi)], axis=1).astype(BF16)
    cmat = jnp.concatenate([out_block(c_re), -out_block(c_im)], axis=1).astype(BF16)
    return bmat, abr.reshape(1, g * p), abi.reshape(1, g * p), cmat


def _rglru_kernel(xg_ref, xr_ref, cw_ref, cb_ref, wa_ref, ba_ref, wx_ref, bx_ref, sp_ref,
                  o_ref, xp_ref, a_ref, b_ref, h_ref, *, batch):
    rows, w = xr_ref.shape
    halo = SUBLANES * ((RG_CONV - 1) * batch // SUBLANES + 1)
    pair = 2 * batch

    @pl.when(pl.program_id(0) == 0)
    def _():
        xp_ref[0:halo, :] = jnp.zeros((halo, w), F32)
        h_ref[...] = jnp.zeros_like(h_ref)

    xr = xr_ref[...]
    xp_ref[halo:halo + rows, :] = xr
    xc = cb_ref[...] + cw_ref[RG_CONV - 1:RG_CONV, :] * xr
    for i in range(RG_CONV - 1):
        back = (RG_CONV - 1 - i) * batch
        xc = xc + cw_ref[i:i + 1, :] * xp_ref[halo - back:halo - back + rows, :]
    xp_ref[0:halo, :] = xr[rows - halo:rows, :]

    xcb = xc.astype(BF16)
    nb = w // RG_BLOCK_W
    ra, ri = [], []
    for hblk in range(nb):
        xs = xcb[:, hblk * RG_BLOCK_W:(hblk + 1) * RG_BLOCK_W]
        ra.append(_dot(xs, wa_ref[hblk].astype(BF16)))
        ri.append(_dot(xs, wx_ref[hblk].astype(BF16)))
    r = _sigmoid(jnp.concatenate(ra, axis=1) + ba_ref[...])
    ig = _sigmoid(jnp.concatenate(ri, axis=1) + bx_ref[...])
    log_a = (-RG_C) * r * sp_ref[...]
    a = jnp.exp(log_a)
    a_ref[...] = a
    b_ref[...] = jnp.sqrt(-jnp.tanh(log_a) * (a * a + 1.0)) * (ig * xc)

    top = lax.broadcasted_iota(jnp.int32, (pair, w), 0) < batch
    unroll = 4

    def body(i, h):
        for s in range(unroll):
            r0 = pl.multiple_of((i * unroll + s) * pair, pair)
            at = a_ref[pl.ds(r0, pair), :]
            bt = b_ref[pl.ds(r0, pair), :]
            n1 = at * h + bt
            h1 = jnp.where(top, n1, pltpu.roll(n1, batch, 0))
            n2 = at * h1 + bt
            b_ref[pl.ds(r0, pair), :] = jnp.where(top, n1, n2)
            h = jnp.where(top, pltpu.roll(n2, batch, 0), n2)
        return h

    h_ref[...] = lax.fori_loop(0, rows // (pair * unroll), body, h_ref[...])
    o_ref[...] = b_ref[...] * _gelu_tanh(xg_ref[...])


def _rglru(proj, cols, conv_w, conv_b, wa, ba, wx, bx, softplus_neg_lam, batch, rows):
    assert 2 * batch == SUBLANES, "two time steps must fill the 8 sublanes"
    n = proj.shape[0]
    w = conv_b.shape[0]
    cg, cx = cols
    nb = wa.shape[0]
    halo = SUBLANES * ((RG_CONV - 1) * batch // SUBLANES + 1)
    vec = pl.BlockSpec((1, w), lambda i: (0, 0))
    blk = pl.BlockSpec((nb, RG_BLOCK_W, RG_BLOCK_W), lambda i: (0, 0, 0))
    return pl.pallas_call(
        functools.partial(_rglru_kernel, batch=batch),
        grid=(n // rows,),
        in_specs=[pl.BlockSpec((rows, w), lambda i: (i, cg)),
                  pl.BlockSpec((rows, w), lambda i: (i, cx)),
                  pl.BlockSpec((RG_CONV, w), lambda i: (0, 0)),
                  vec, blk, vec, blk, vec, vec],
        out_specs=pl.BlockSpec((rows, w), lambda i: (i, 0)),
        out_shape=jax.ShapeDtypeStruct((n, w), F32),
        scratch_shapes=[pltpu.VMEM((halo + rows, w), F32),
                        pltpu.VMEM((rows, w), F32),
                        pltpu.VMEM((rows, w), F32),
                        pltpu.VMEM((2 * batch, w), F32)],
        compiler_params=_cparams(("arbitrary",)),
        name="rglru",
    )(proj, proj, conv_w, conv_b.reshape(1, w), wa, ba.reshape(1, w), wx, bx.reshape(1, w),
      softplus_neg_lam.reshape(1, w))


def _merge_kernel(x_ref, ya_ref, yb_ref, yc_ref, *rest, alpha, halves):
    n_gate = 3 * halves
    gate_refs = rest[:n_gate]
    wb_ref, bg_ref, wo_ref, lg_ref, lb_ref, o_ref = rest[n_gate:]
    m = None
    for kbr, y_ref in enumerate((ya_ref, yb_ref, yc_ref)):
        br = _dot(y_ref[...].astype(BF16), wb_ref[kbr].astype(BF16))
        gp = jnp.concatenate([gate_refs[kbr * halves + i][...] for i in range(halves)], axis=1)
        t = _sigmoid(gp + bg_ref[kbr:kbr + 1, :]) * br
        m = t if m is None else m + t
    mix = _dot(m.astype(BF16), wo_ref[...].astype(BF16))
    o_ref[...] = _layernorm(alpha * x_ref[...] + mix, lg_ref[...], lb_ref[...])


def _merge(x, ya, yb, yc, proj, gate_col0, w_branch, b_gate, w_out, ln_g, ln_b, alpha, tm):
    n, d = x.shape
    w = ya.shape[1]
    nbr = w_branch.shape[0]
    halves = d // w
    row = lambda width: pl.BlockSpec((tm, width), lambda i: (i, 0))
    gates = [pl.BlockSpec((tm, w), lambda i, c=gate_col0 + c: (i, c)) for c in range(nbr * halves)]
    vec = pl.BlockSpec((1, d), lambda i: (0, 0))
    return pl.pallas_call(
        functools.partial(_merge_kernel, alpha=alpha, halves=halves),
        grid=(n // tm,),
        in_specs=[row(d), row(w), row(w), row(w)] + gates + [
            pl.BlockSpec((nbr, w, d), lambda i: (0, 0, 0)),
            pl.BlockSpec((nbr, d), lambda i: (0, 0)),
            pl.BlockSpec((d, d), lambda i: (0, 0)), vec, vec],
        out_specs=row(d),
        out_shape=jax.ShapeDtypeStruct((n, d), F32),
        compiler_params=_cparams(("parallel",)),
        name="merge",
    )(x, ya, yb, yc, *([proj] * (nbr * halves)), w_branch, b_gate.reshape(nbr, d), w_out,
      ln_g.reshape(1, d), ln_b.reshape(1, d))


def _first_max_mask(cur, idx, axis):
    m = jnp.max(cur, axis=axis, keepdims=True)
    first = jnp.min(jnp.where(cur == m, idx, jnp.int32(2 ** 30)), axis=axis, keepdims=True)
    return idx == first


def _router_kernel(x_ref, wr_ref, rb_ref, o_ref):
    tm = x_ref.shape[0]
    e = N_EXPERTS
    per = e // N_EXPERT_GROUPS
    x = x_ref[...]
    wr = wr_ref[...]
    xh = x.astype(BF16)
    xl = (x - xh.astype(F32)).astype(BF16)
    wh = wr.astype(BF16)
    wl = (wr - wh.astype(F32)).astype(BF16)
    nt = (((1,), (1,)), ((), ()))
    logits = (lax.dot_general(wh, xh, nt, preferred_element_type=F32)
              + lax.dot_general(wh, xl, nt, preferred_element_type=F32)
              + lax.dot_general(wl, xh, nt, preferred_element_type=F32))
    scores = _sigmoid(logits)
    biased = scores + rb_ref[...]
    neg = jnp.float32(-jnp.inf)

    b3 = biased.reshape(N_EXPERT_GROUPS, per, tm)
    i3 = lax.broadcasted_iota(jnp.int32, b3.shape, 1)
    top1 = _first_max_mask(b3, i3, 1)
    m1 = jnp.max(b3, axis=1, keepdims=True)
    m2 = jnp.max(jnp.where(top1, neg, b3), axis=1, keepdims=True)
    gscore = (m1 + m2).reshape(N_EXPERT_GROUPS, tm)

    ig = lax.broadcasted_iota(jnp.int32, gscore.shape, 0)
    gsel = jnp.zeros(gscore.shape, F32)
    cur = gscore
    for _ in range(TOPK_GROUPS):
        pick = _first_max_mask(cur, ig, 0)
        gsel = jnp.where(pick, 1.0, gsel)
        cur = jnp.where(pick, neg, cur)

    gsel3 = jnp.broadcast_to(gsel.reshape(N_EXPERT_GROUPS, 1, tm), b3.shape)
    masked = jnp.where(gsel3 > 0.0, b3, neg).reshape(e, tm)
    ie = lax.broadcasted_iota(jnp.int32, masked.shape, 0)
    chosen = jnp.zeros(masked.shape, F32)
    cur = masked
    for _ in range(TOP_K):
        pick = _first_max_mask(cur, ie, 0)
        chosen = jnp.where(pick, 1.0, chosen)
        cur = jnp.where(pick, neg, cur)

    wsel = jnp.where(chosen > 0.0, scores, 0.0)
    wsel = wsel / jnp.sum(wsel, axis=0, keepdims=True) * ROUTED_SCALE
    wt = jnp.concatenate([wsel, jnp.zeros((LANES - e, tm), F32)], axis=0).T
    hi = wt.astype(BF16)
    lo = (wt - hi.astype(F32)).astype(BF16)
    o_ref[...] = jnp.concatenate([hi, lo], axis=1)


def _router(x, router_w_t, router_bias, tm):
    n, d = x.shape
    e = router_w_t.shape[0]
    return pl.pallas_call(
        _router_kernel,
        grid=(n // tm,),
        in_specs=[pl.BlockSpec((tm, d), lambda i: (i, 0)),
                  pl.BlockSpec((e, d), lambda i: (0, 0)),
                  pl.BlockSpec((e, 1), lambda i: (0, 0))],
        out_specs=pl.BlockSpec((tm, 2 * LANES), lambda i: (i, 0)),
        out_shape=jax.ShapeDtypeStruct((n, 2 * LANES), BF16),
        compiler_params=_cparams(("parallel",)),
        name="router",
    )(x, router_w_t, router_bias.reshape(e, 1))


def _moe_kernel(x_ref, wc_ref, w1_ref, w3_ref, w2_ref, s1_ref, s3_ref, s2_ref, lg_ref, lb_ref,
                o_ref, xb_ref, acc_ref, *, alpha):
    e = pl.program_id(1)
    hid = w1_ref.shape[2]

    @pl.when(e == 0)
    def _():
        xb_ref[...] = x_ref[...].astype(BF16)
        acc_ref[...] = jnp.zeros_like(acc_ref)

    xb = xb_ref[...]
    h1 = _dot(xb, w1_ref[0].astype(BF16))
    h3 = _dot(xb, w3_ref[0].astype(BF16))
    srow = lax.broadcasted_iota(jnp.int32, (2 * LANES, hid), 0)
    sel = jnp.where(srow % LANES == e, 1.0, 0.0).astype(BF16)
    wcol = _dot(wc_ref[...], sel)
    hh = _silu(h1) * h3 * wcol
    acc_ref[...] += _dot(hh.astype(BF16), w2_ref[0].astype(BF16))

    @pl.when(e == pl.num_programs(1) - 1)
    def _():
        g1 = _dot(xb, s1_ref[...].astype(BF16))
        g3 = _dot(xb, s3_ref[...].astype(BF16))
        sh = _dot((_silu(g1) * g3).astype(BF16), s2_ref[...].astype(BF16))
        z = alpha * x_ref[...] + (acc_ref[...] + sh)
        o_ref[...] = _layernorm(z, lg_ref[...], lb_ref[...])


def _moe(x, wcomb, w1, w3, w2, s1, s3, s2, ln_g, ln_b, alpha, tm):
    n, d = x.shape
    e, _, hid = w1.shape
    vec = pl.BlockSpec((1, d), lambda i, j: (0, 0))
    return pl.pallas_call(
        functools.partial(_moe_kernel, alpha=alpha),
        grid=(n // tm, e),
        in_specs=[pl.BlockSpec((tm, d), lambda i, j: (i, 0)),
                  pl.BlockSpec((tm, 2 * LANES), lambda i, j: (i, 0)),
                  pl.BlockSpec((1, d, hid), lambda i, j: (j, 0, 0)),
                  pl.BlockSpec((1, d, hid), lambda i, j: (j, 0, 0)),
                  pl.BlockSpec((1, hid, d), lambda i, j: (j, 0, 0)),
                  pl.BlockSpec((d, hid), lambda i, j: (0, 0)),
                  pl.BlockSpec((d, hid), lambda i, j: (0, 0)),
                  pl.BlockSpec((hid, d), lambda i, j: (0, 0)), vec, vec],
        out_specs=pl.BlockSpec((tm, d), lambda i, j: (i, 0)),
        out_shape=jax.ShapeDtypeStruct((n, d), F32),
        scratch_shapes=[pltpu.VMEM((tm, d), BF16), pltpu.VMEM((tm, d), F32)],
        compiler_params=_cparams(("parallel", "arbitrary")),
        name="moe",
    )(x, wcomb, w1, w3, w2, s1, s3, s2, ln_g.reshape(1, d), ln_b.reshape(1, d))


def kernel(x, w_in, b_gate, hgrn_lb_logits, hgrn_norm_w, s5_a_re, s5_a_im, s5_b_re, s5_b_im,
           s5_c_re, s5_c_im, s5_d, s5_log_dt, s5_glu_w, s5_glu_b, rg_conv_w, rg_conv_b,
           rg_wa, rg_ba, rg_wx, rg_bx, rg_lambda, w_branch, w_out, ln1_g, ln1_b,
           router_w, router_bias, exp_w1, exp_w3, exp_w2, sh_w1, sh_w3, sh_w2, ln2_g, ln2_b):
    bn, s, d = x.shape
    depth = w_in.shape[0]
    n = bn * s
    w = hgrn_norm_w.shape[1]
    alpha = (2 * depth) ** 0.25

    sm = jax.nn.softmax(hgrn_lb_logits.astype(F32), axis=0)
    lower_bounds = jnp.cumsum(sm, axis=0) - sm[0:1]

    xt = x.transpose(1, 0, 2).reshape(n, d)
    seq_rows = 256 * bn
    c_hg = (0, 1, 2, 3)
    c_su = 4
    c_rg = (5, 6)
    c_gate = 7
    for l in range(depth):
        proj = _matmul(xt, w_in[l], 1024, w)
        ya = _hgrn2(proj, c_hg, lower_bounds[l], hgrn_norm_w[l], bn, seq_rows)
        mats = _s5_matrices(s5_a_re[l], s5_a_im[l], s5_b_re[l], s5_b_im[l],
                            s5_c_re[l], s5_c_im[l], s5_log_dt[l])
        yb = _s5(proj, c_su, mats, s5_d[l], s5_glu_w[l], s5_glu_b[l], bn, seq_rows // 2)
        yc = _rglru(proj, c_rg, rg_conv_w[l], rg_conv_b[l], rg_wa[l], rg_ba[l], rg_wx[l], rg_bx[l],
                    jax.nn.softplus(-rg_lambda[l].astype(F32)), bn, seq_rows)
        x1 = _merge(xt, ya, yb, yc, proj, c_gate, w_branch[l], b_gate[l], w_out[l],
                    ln1_g[l], ln1_b[l], alpha, 512)
        wcomb = _router(x1, router_w[l].T, router_bias[l], 512)
        xt = _moe(x1, wcomb, exp_w1[l], exp_w3[l], exp_w2[l], sh_w1[l], sh_w3[l], sh_w2[l],
                  ln2_g[l], ln2_b[l], alpha, 1024)
    return xt.reshape(s, bn, d).transpose(1, 0, 2)
```

```python
import functools
import math

import numpy as np
import jax
import jax.numpy as jnp
from jax import lax
from jax.experimental import pallas as pl
from jax.experimental.pallas import tpu as pltpu

F32 = jnp.float32
BF16 = jnp.bfloat16

HG_HEADS = 4
HG_HEAD_DIM = 128
HG_CHUNK = 16
S5_GROUP_CH = 16
S5_STATE = 64
RG_BLOCK_W = 128
RG_CONV = 4
RG_C = 8.0
N_EXPERTS = 64
N_EXPERT_GROUPS = 8
TOPK_GROUPS = 4
TOP_K = 8
ROUTED_SCALE = 2.5
LN_EPS = 1e-5
RMS_EPS = 1e-6
LANES = 128
SUBLANES = 8
VMEM_LIMIT = 56 * 1024 * 1024
SUB_ROWS = 16
MOE_TILE = 512
FFN_ROWS = 512
MM_ROWS = 512


def _cparams(sem):
    return pltpu.CompilerParams(dimension_semantics=sem, vmem_limit_bytes=VMEM_LIMIT)


def _sigmoid(x):
    return 1.0 / (1.0 + jnp.exp(-x))


def _silu(x):
    return x * _sigmoid(x)


def _gelu_tanh(x):
    c = math.sqrt(2.0 / math.pi)
    return 0.5 * x * (1.0 + jnp.tanh(c * (x + 0.044715 * (x * x * x))))


def _layernorm(z, g, b):
    mu = jnp.mean(z, axis=-1, keepdims=True)
    zc = z - mu
    var = jnp.mean(zc * zc, axis=-1, keepdims=True)
    return zc * lax.rsqrt(var + LN_EPS) * g + b


def _dot(a, b):
    return jnp.dot(a, b, preferred_element_type=F32)


def _matmul_kernel(x_ref, w_ref, o_ref, xb_ref):
    @pl.when(pl.program_id(1) == 0)
    def _():
        xb_ref[...] = x_ref[...].astype(BF16)

    o_ref[...] = _dot(xb_ref[...], w_ref[0].astype(BF16)).astype(o_ref.dtype)


def _matmul(x, w, layer, tm, tn):
    m, k = x.shape
    n = w.shape[2]
    return pl.pallas_call(
        _matmul_kernel,
        grid=(m // tm, n // tn),
        in_specs=[pl.BlockSpec((tm, k), lambda i, j: (i, 0)),
                  pl.BlockSpec((1, k, tn), lambda i, j: (layer, 0, j))],
        out_specs=pl.BlockSpec((tm, tn), lambda i, j: (i, j)),
        out_shape=jax.ShapeDtypeStruct((m, n), F32),
        scratch_shapes=[pltpu.VMEM((tm, k), BF16)],
        compiler_params=_cparams(("parallel", "arbitrary")),
        name="in_proj",
    )(x, w)


def _hgrn2_kernel(q_ref, f_ref, v_ref, g_ref, lb_ref, nw_ref, o_ref, st_ref, *, batch):
    rows = q_ref.shape[0]
    cr = HG_CHUNK * batch
    n_chunks = rows // cr
    dh = HG_HEAD_DIM

    @pl.when(pl.program_id(0) == 0)
    def _():
        st_ref[...] = jnp.zeros_like(st_ref)

    row = lax.broadcasted_iota(jnp.int32, (cr, dh), 0)
    row_b = row % batch
    ones_sum = jnp.ones((dh, dh), BF16)

    def chunk(c, carry):
        r0 = pl.multiple_of(c * cr, cr)
        for h in range(HG_HEADS):
            ls = slice(h * dh, (h + 1) * dh)
            lb = lb_ref[:, ls]
            f = lb + (1.0 - lb) * _sigmoid(f_ref[pl.ds(r0, cr), ls])
            q = _silu(q_ref[pl.ds(r0, cr), ls])
            k = 1.0 - f
            v = v_ref[pl.ds(r0, cr), ls]
            bc = jnp.log(f)
            sh = batch
            while sh < cr:
                bc = bc + jnp.where(row >= sh, pltpu.roll(bc, sh, 0), 0.0)
                sh *= 2
            p0 = (q * k).astype(BF16)
            o = _dot(p0, ones_sum) * v
            for j in range(1, HG_CHUNK):
                s = j * batch
                dec = jnp.exp(bc - pltpu.roll(bc, s, 0))
                p = jnp.where(row >= s, q * pltpu.roll(k, s, 0) * dec, 0.0)
                o = o + _dot(p.astype(BF16), ones_sum) * pltpu.roll(v, s, 0)
            b_last = bc[cr - batch:cr, :]
            qt = q * jnp.exp(bc)
            kt = k * jnp.exp(jnp.concatenate([b_last] * HG_CHUNK, axis=0) - bc)
            qm = jnp.concatenate([jnp.where(row_b == b, qt, 0.0) for b in range(batch)], axis=1)
            km = jnp.concatenate([jnp.where(row_b == b, kt, 0.0) for b in range(batch)], axis=1)
            st = st_ref[h]
            o = o + lax.dot_general(qm.astype(BF16), st.astype(BF16),
                                    (((1,), (1,)), ((), ())), preferred_element_type=F32)
            kv = lax.dot_general(v.astype(BF16), km.astype(BF16),
                                 (((0,), (0,)), ((), ())), preferred_element_type=F32)
            dec_all = jnp.concatenate([jnp.exp(b_last[b:b + 1, :]) for b in range(batch)], axis=1)
            st_ref[h] = st * dec_all + kv
            o = o * lax.rsqrt(jnp.mean(o * o, axis=-1, keepdims=True) + RMS_EPS)
            o_ref[pl.ds(r0, cr), ls] = o * nw_ref[:, ls] * _silu(g_ref[pl.ds(r0, cr), ls])
        return carry

    lax.fori_loop(0, n_chunks, chunk, 0)


def _hgrn2(proj, cols, lb, norm_w, batch, rows):
    n = proj.shape[0]
    w = HG_HEADS * HG_HEAD_DIM
    cq, cf, cv, cg = cols

    def spec(cb):
        return pl.BlockSpec((rows, w), lambda i: (i, cb))

    vec = pl.BlockSpec((1, w), lambda i: (0, 0))
    return pl.pallas_call(
        functools.partial(_hgrn2_kernel, batch=batch),
        grid=(n // rows,),
        in_specs=[spec(cq), spec(cf), spec(cv), spec(cg), vec, vec],
        out_specs=pl.BlockSpec((rows, w), lambda i: (i, 0)),
        out_shape=jax.ShapeDtypeStruct((n, w), F32),
        scratch_shapes=[pltpu.VMEM((HG_HEADS, HG_HEAD_DIM, batch * HG_HEAD_DIM), F32)],
        compiler_params=_cparams(("arbitrary",)),
        name="hgrn2",
    )(proj, proj, proj, proj, lb.reshape(1, w), norm_w.reshape(1, w))


S5_TILE = 128


def _s5_kernel(u_ref, pin_ref, bm_ref, ar_ref, ai_ref, qout_ref, cm_ref, d_ref, gw_ref, gb_ref,
               o_ref, hs_ref, h_ref, *, batch):
    rows, w = u_ref.shape
    nblk = w // LANES
    nc = hs_ref.shape[1]
    cpb = nc // nblk
    tile2 = 2 * S5_TILE
    n_tiles = rows // S5_TILE
    steps = rows // batch

    @pl.when(pl.program_id(0) == 0)
    def _():
        h_ref[...] = jnp.zeros_like(h_ref)

    for t in range(n_tiles):
        ub = u_ref[t * S5_TILE:(t + 1) * S5_TILE, :].astype(BF16)
        up = _dot(pin_ref[...], ub).astype(BF16)
        for j in range(nblk):
            ls = slice(j * LANES, (j + 1) * LANES)
            lhs = jnp.concatenate([up[0:tile2, ls], up[tile2:2 * tile2, ls]], axis=1)
            hs_ref[t * tile2:(t + 1) * tile2, j * cpb:(j + 1) * cpb] = _dot(lhs, bm_ref[j])

    top = lax.broadcasted_iota(jnp.int32, (2 * batch, LANES), 0) < batch
    groups = nc // LANES
    per_pass = 8
    unroll = 4
    for p in range(groups // per_pass):
        cols = [(p * per_pass + g) * LANES for g in range(per_pass)]
        ar8 = [jnp.broadcast_to(ar_ref[:, c:c + LANES], (2 * batch, LANES)) for c in cols]
        ai8 = [jnp.where(top, -1.0, 1.0) * ai_ref[:, c:c + LANES] for c in cols]

        def body(i, hs):
            hs = list(hs)
            for s in range(unroll):
                r0 = pl.multiple_of((i * unroll + s) * (2 * batch), 2 * batch)
                for g, c in enumerate(cols):
                    h = (ar8[g] * hs[g] + ai8[g] * pltpu.roll(hs[g], batch, 0)
                         + hs_ref[pl.ds(r0, 2 * batch), c:c + LANES])
                    hs_ref[pl.ds(r0, 2 * batch), c:c + LANES] = h
                    hs[g] = h
            return tuple(hs)

        h0 = tuple(h_ref[:, c:c + LANES] for c in cols)
        hn = lax.fori_loop(0, steps // unroll, body, h0)
        for g, c in enumerate(cols):
            h_ref[:, c:c + LANES] = hn[g]

    for t in range(n_tiles):
        hb = hs_ref[t * tile2:(t + 1) * tile2, :].astype(BF16)
        hre = _dot(qout_ref[0], hb).astype(BF16)
        him = _dot(qout_ref[1], hb).astype(BF16)
        ys = []
        for j in range(nblk):
            cs = slice(j * cpb, (j + 1) * cpb)
            ys.append(_dot(jnp.concatenate([hre[:, cs], him[:, cs]], axis=1), cm_ref[j]))
        rs = slice(t * S5_TILE, (t + 1) * S5_TILE)
        y = jnp.concatenate(ys, axis=1) + d_ref[...] * u_ref[rs, :]
        y = _gelu_tanh(y)
        z = _dot(y.astype(BF16), gw_ref[...].astype(BF16)) + gb_ref[...]
        o_ref[rs, :] = y * _sigmoid(z)


def _s5_layout_matrices(batch):
    assert 2 * batch == SUBLANES, "one time step must fill the 8 sublanes"
    tile2 = 2 * S5_TILE
    r2 = np.arange(tile2)
    src = (r2 // (2 * batch)) * batch + r2 % batch
    is_re = (r2 % (2 * batch)) < batch
    onehot = (src[:, None] == np.arange(S5_TILE)[None, :])
    p_re = onehot & is_re[:, None]
    p_im = onehot & ~is_re[:, None]
    pin = np.concatenate([p_re, p_im], axis=0).astype(np.float32)
    qout = np.stack([p_re.T, p_im.T], axis=0).astype(np.float32)
    return jnp.asarray(pin, BF16), jnp.asarray(qout, BF16)


def _s5(proj, col, mats, d, glu_w, glu_b, batch, rows):
    n = proj.shape[0]
    w = d.shape[0]
    bmat, ar, ai, cmat = mats
    nblk, _, cpb = bmat.shape
    nc = nblk * cpb
    pin, qout = _s5_layout_matrices(batch)
    full = lambda a: pl.BlockSpec(a.shape, lambda i, nd=a.ndim: (0,) * nd)
    return pl.pallas_call(
        functools.partial(_s5_kernel, batch=batch),
        grid=(n // rows,),
        in_specs=[pl.BlockSpec((rows, w), lambda i: (i, col)),
                  full(pin), full(bmat), full(ar), full(ai), full(qout), full(cmat),
                  pl.BlockSpec((1, w), lambda i: (0, 0)),
                  pl.BlockSpec((w, w), lambda i: (0, 0)),
                  pl.BlockSpec((1, w), lambda i: (0, 0))],
        out_specs=pl.BlockSpec((rows, w), lambda i: (i, 0)),
        out_shape=jax.ShapeDtypeStruct((n, w), F32),
        scratch_shapes=[pltpu.VMEM((2 * rows, nc), F32),
                        pltpu.VMEM((2 * batch, nc), F32)],
        compiler_params=_cparams(("arbitrary",)),
        name="s5",
    )(proj, pin, bmat, ar, ai, qout, cmat, d.reshape(1, w), glu_w, glu_b.reshape(1, w))


def _s5_matrices(a_re, a_im, b_re, b_im, c_re, c_im, log_dt):
    g, p = a_re.shape
    ch = b_re.shape[-1]
    gpb = LANES // ch
    nblk = g // gpb
    dt = jnp.exp(log_dt.astype(F32))[:, None]
    mag = jnp.exp(a_re * dt)
    abr = mag * jnp.cos(a_im * dt)
    abi = mag * jnp.sin(a_im * dt)
    den = a_re * a_re + a_im * a_im
    cr = ((abr - 1.0) * a_re + abi * a_im) / den
    ci = (abi * a_re - (abr - 1.0) * a_im) / den
    bbr = cr[..., None] * b_re - ci[..., None] * b_im
    bbi = cr[..., None] * b_im + ci[..., None] * b_re
    eye = jnp.eye(gpb, dtype=F32)

    def in_block(m):
        m = m.reshape(nblk, gpb, p, ch)
        return jnp.einsum('jgpc,gh->jgchp', m, eye).reshape(nblk, gpb * ch, gpb * p)

    def out_block(m):
        m = m.reshape(nblk, gpb, ch, p)
        return jnp.einsum('jgcp,gh->jgphc', m, eye).reshape(nblk, gpb * p, gpb * ch)

    bmat = jnp.concatenate([in_block(bbr), in_block(bbi)], axis=1).astype(BF16)
    cmat = jnp.concatenate([out_block(c_re), -out_block(c_im)], axis=1).astype(BF16)
    return bmat, abr.reshape(1, g * p), abi.reshape(1, g * p), cmat


def _rglru_kernel(xg_ref, xr_ref, cw_ref, cb_ref, wa_ref, ba_ref, wx_ref, bx_ref, sp_ref,
                  o_ref, xp_ref, a_ref, b_ref, h_ref, *, batch):
    rows, w = xr_ref.shape
    halo = SUBLANES * ((RG_CONV - 1) * batch // SUBLANES + 1)
    pair = 2 * batch

    @pl.when(pl.program_id(0) == 0)
    def _():
        xp_ref[0:halo, :] = jnp.zeros((halo, w), F32)
        h_ref[...] = jnp.zeros_like(h_ref)

    xr = xr_ref[...]
    xp_ref[halo:halo + rows, :] = xr
    xc = cb_ref[...] + cw_ref[RG_CONV - 1:RG_CONV, :] * xr
    for i in range(RG_CONV - 1):
        back = (RG_CONV - 1 - i) * batch
        xc = xc + cw_ref[i:i + 1, :] * xp_ref[halo - back:halo - back + rows, :]
    xp_ref[0:halo, :] = xr[rows - halo:rows, :]

    xcb = xc.astype(BF16)
    nb = w // RG_BLOCK_W
    ra, ri = [], []
    for hblk in range(nb):
        xs = xcb[:, hblk * RG_BLOCK_W:(hblk + 1) * RG_BLOCK_W]
        ra.append(_dot(xs, wa_ref[hblk].astype(BF16)))
        ri.append(_dot(xs, wx_ref[hblk].astype(BF16)))
    r = _sigmoid(jnp.concatenate(ra, axis=1) + ba_ref[...])
    ig = _sigmoid(jnp.concatenate(ri, axis=1) + bx_ref[...])
    log_a = (-RG_C) * r * sp_ref[...]
    a = jnp.exp(log_a)
    a_ref[...] = a
    b_ref[...] = jnp.sqrt(-jnp.tanh(log_a) * (a * a + 1.0)) * (ig * xc)

    top = lax.broadcasted_iota(jnp.int32, (pair, w), 0) < batch
    unroll = 4

    def body(i, h):
        for s in range(unroll):
            r0 = pl.multiple_of((i * unroll + s) * pair, pair)
            at = a_ref[pl.ds(r0, pair), :]
            bt = b_ref[pl.ds(r0, pair), :]
            n1 = at * h + bt
            h1 = jnp.where(top, n1, pltpu.roll(n1, batch, 0))
            n2 = at * h1 + bt
            b_ref[pl.ds(r0, pair), :] = jnp.where(top, n1, n2)
            h = jnp.where(top, pltpu.roll(n2, batch, 0), n2)
        return h

    h_ref[...] = lax.fori_loop(0, rows // (pair * unroll), body, h_ref[...])
    o_ref[...] = b_ref[...] * _gelu_tanh(xg_ref[...])


def _rglru(proj, cols, conv_w, conv_b, wa, ba, wx, bx, softplus_neg_lam, batch, rows):
    assert 2 * batch == SUBLANES, "two time steps must fill the 8 sublanes"
    n = proj.shape[0]
    w = conv_b.shape[0]
    cg, cx = cols
    nb = wa.shape[0]
    halo = SUBLANES * ((RG_CONV - 1) * batch // SUBLANES + 1)
    vec = pl.BlockSpec((1, w), lambda i: (0, 0))
    blk = pl.BlockSpec((nb, RG_BLOCK_W, RG_BLOCK_W), lambda i: (0, 0, 0))
    return pl.pallas_call(
        functools.partial(_rglru_kernel, batch=batch),
        grid=(n // rows,),
        in_specs=[pl.BlockSpec((rows, w), lambda i: (i, cg)),
                  pl.BlockSpec((rows, w), lambda i: (i, cx)),
                  pl.BlockSpec((RG_CONV, w), lambda i: (0, 0)),
                  vec, blk, vec, blk, vec, vec],
        out_specs=pl.BlockSpec((rows, w), lambda i: (i, 0)),
        out_shape=jax.ShapeDtypeStruct((n, w), F32),
        scratch_shapes=[pltpu.VMEM((halo + rows, w), F32),
                        pltpu.VMEM((rows, w), F32),
                        pltpu.VMEM((rows, w), F32),
                        pltpu.VMEM((2 * batch, w), F32)],
        compiler_params=_cparams(("arbitrary",)),
        name="rglru",
    )(proj, proj, conv_w, conv_b.reshape(1, w), wa, ba.reshape(1, w), wx, bx.reshape(1, w),
      softplus_neg_lam.reshape(1, w))


def _merge_kernel(x_ref, ya_ref, yb_ref, yc_ref, *rest, alpha, halves):
    n_gate = 3 * halves
    gate_refs = rest[:n_gate]
    wb_ref, bg_ref, wo_ref, lg_ref, lb_ref, o_ref = rest[n_gate:]
    m = None
    for kbr, y_ref in enumerate((ya_ref, yb_ref, yc_ref)):
        br = _dot(y_ref[...].astype(BF16), wb_ref[kbr].astype(BF16))
        gp = jnp.concatenate([gate_refs[kbr * halves + i][...] for i in range(halves)], axis=1)
        t = _sigmoid(gp + bg_ref[kbr:kbr + 1, :]) * br
        m = t if m is None else m + t
    mix = _dot(m.astype(BF16), wo_ref[...].astype(BF16))
    o_ref[...] = _layernorm(alpha * x_ref[...] + mix, lg_ref[...], lb_ref[...])


def _merge(x, ya, yb, yc, proj, gate_col0, w_branch, b_gate, w_out, ln_g, ln_b, alpha, tm):
    n, d = x.shape
    w = ya.shape[1]
    nbr = w_branch.shape[0]
    halves = d // w
    row = lambda width: pl.BlockSpec((tm, width), lambda i: (i, 0))
    gates = [pl.BlockSpec((tm, w), lambda i, c=gate_col0 + c: (i, c)) for c in range(nbr * halves)]
    vec = pl.BlockSpec((1, d), lambda i: (0, 0))
    return pl.pallas_call(
        functools.partial(_merge_kernel, alpha=alpha, halves=halves),
        grid=(n // tm,),
        in_specs=[row(d), row(w), row(w), row(w)] + gates + [
            pl.BlockSpec((nbr, w, d), lambda i: (0, 0, 0)),
            pl.BlockSpec((nbr, d), lambda i: (0, 0)),
            pl.BlockSpec((d, d), lambda i: (0, 0)), vec, vec],
        out_specs=row(d),
        out_shape=jax.ShapeDtypeStruct((n, d), F32),
        compiler_params=_cparams(("parallel",)),
        name="merge",
    )(x, ya, yb, yc, *([proj] * (nbr * halves)), w_branch, b_gate.reshape(nbr, d), w_out,
      ln_g.reshape(1, d), ln_b.reshape(1, d))


def _first_max_mask(cur, idx, axis):
    m = jnp.max(cur, axis=axis, keepdims=True)
    first = jnp.min(jnp.where(cur == m, idx, jnp.int32(2 ** 30)), axis=axis, keepdims=True)
    return idx == first


def _router_kernel(x_ref, wr_ref, rb_ref, pos_ref, w_ref, seg_ref):
    tm = x_ref.shape[0]
    e = N_EXPERTS
    per = e // N_EXPERT_GROUPS
    x = x_ref[...]
    wr = wr_ref[...]
    xh = x.astype(BF16)
    xl = (x - xh.astype(F32)).astype(BF16)
    wh = wr.astype(BF16)
    wl = (wr - wh.astype(F32)).astype(BF16)
    nt = (((1,), (1,)), ((), ()))
    logits = (lax.dot_general(wh, xh, nt, preferred_element_type=F32)
              + lax.dot_general(wh, xl, nt, preferred_element_type=F32)
              + lax.dot_general(wl, xh, nt, preferred_element_type=F32))
    scores = _sigmoid(logits)
    biased = scores + rb_ref[...]
    neg = jnp.float32(-jnp.inf)

    b3 = biased.reshape(N_EXPERT_GROUPS, per, tm)
    i3 = lax.broadcasted_iota(jnp.int32, b3.shape, 1)
    top1 = _first_max_mask(b3, i3, 1)
    m1 = jnp.max(b3, axis=1, keepdims=True)
    m2 = jnp.max(jnp.where(top1, neg, b3), axis=1, keepdims=True)
    gscore = (m1 + m2).reshape(N_EXPERT_GROUPS, tm)

    ig = lax.broadcasted_iota(jnp.int32, gscore.shape, 0)
    gsel = jnp.zeros(gscore.shape, F32)
    cur = gscore
    for _ in range(TOPK_GROUPS):
        pick = _first_max_mask(cur, ig, 0)
        gsel = jnp.where(pick, 1.0, gsel)
        cur = jnp.where(pick, neg, cur)

    gsel3 = jnp.broadcast_to(gsel.reshape(N_EXPERT_GROUPS, 1, tm), b3.shape)
    masked = jnp.where(gsel3 > 0.0, b3, neg).reshape(e, tm)
    ie = lax.broadcasted_iota(jnp.int32, masked.shape, 0)
    chosen = jnp.zeros(masked.shape, F32)
    cur = masked
    for _ in range(TOP_K):
        pick = _first_max_mask(cur, ie, 0)
        chosen = jnp.where(pick, 1.0, chosen)
        cur = jnp.where(pick, neg, cur)

    wsel = jnp.where(chosen > 0.0, scores, 0.0)
    wsel = wsel / jnp.sum(wsel, axis=0, keepdims=True) * ROUTED_SCALE
    w_ref[0] = wsel

    cb = chosen.astype(BF16)
    tok_r = lax.broadcasted_iota(jnp.int32, (tm, tm), 0)
    tok_c = lax.broadcasted_iota(jnp.int32, (tm, tm), 1)
    before = jnp.where(tok_r < tok_c, 1.0, 0.0).astype(BF16)
    prefix = _dot(cb, before)
    cnt = jnp.sum(chosen, axis=1, keepdims=True)
    seg = jnp.floor((cnt + (SUB_ROWS - 1)) * (1.0 / SUB_ROWS))
    seg_b = jnp.broadcast_to(seg, (e, LANES))
    ex_r = lax.broadcasted_iota(jnp.int32, (e, e), 0)
    ex_c = lax.broadcasted_iota(jnp.int32, (e, e), 1)
    earlier = jnp.where(ex_c < ex_r, 1.0, 0.0).astype(BF16)
    seg_off = _dot(earlier, seg_b.astype(BF16))
    seg_ref[0] = seg_b
    pos_ref[0] = jnp.where(chosen > 0.0, seg_off[:, 0:1] * SUB_ROWS + prefix, -1.0)


def _router(x, router_w_t, router_bias, tm):
    n, d = x.shape
    e = router_w_t.shape[0]
    nt = n // tm
    tile = lambda width: pl.BlockSpec((1, e, width), lambda i: (i, 0, 0))
    return pl.pallas_call(
        _router_kernel,
        grid=(nt,),
        in_specs=[pl.BlockSpec((tm, d), lambda i: (i, 0)),
                  pl.BlockSpec((e, d), lambda i: (0, 0)),
                  pl.BlockSpec((e, 1), lambda i: (0, 0))],
        out_specs=[tile(tm), tile(tm), tile(LANES)],
        out_shape=[jax.ShapeDtypeStruct((nt, e, tm), F32),
                   jax.ShapeDtypeStruct((nt, e, tm), F32),
                   jax.ShapeDtypeStruct((nt, e, LANES), F32)],
        compiler_params=_cparams(("parallel",)),
        name="router",
    )(x, router_w_t, router_bias.reshape(e, 1))


def _tile_buffer_rows(tile, n_experts, top_k):
    rows = tile * top_k + n_experts * (SUB_ROWS - 1)
    return -(-rows // MM_ROWS) * MM_ROWS


def _slot_matrix(j, e, pos_ref, val_ref, tile):
    e = jnp.maximum(e, 0)
    prow = pos_ref[0, pl.ds(e, 1), :]
    tgt = (lax.broadcasted_iota(jnp.int32, (SUB_ROWS, tile), 0) + j * SUB_ROWS).astype(F32)
    if val_ref is None:
        return jnp.where(prow == tgt, 1.0, 0.0).astype(BF16)
    return jnp.where(prow == tgt, val_ref[0, pl.ds(e, 1), :], 0.0).astype(BF16)


def _dispatch_kernel(sub_e_ref, sub_dst_ref, x_ref, pos_ref, init_ref, xs_hbm, m_ref, xs_ref, sem):
    del init_ref
    t = pl.program_id(0)
    tile = x_ref.shape[0]
    n_sub = m_ref.shape[0] // SUB_ROWS
    base = t * n_sub

    def build(j, c):
        r0 = pl.multiple_of(j * SUB_ROWS, SUB_ROWS)
        m_ref[pl.ds(r0, SUB_ROWS), :] = _slot_matrix(j, sub_e_ref[base + j], pos_ref, None, tile)
        return c

    lax.fori_loop(0, n_sub, build, 0)
    xb = x_ref[...].astype(BF16)
    for c in range(m_ref.shape[0] // MM_ROWS):
        rs = slice(c * MM_ROWS, (c + 1) * MM_ROWS)
        xs_ref[rs, :] = _dot(m_ref[rs, :], xb).astype(BF16)

    def copy(j):
        r0 = pl.multiple_of(j * SUB_ROWS, SUB_ROWS)
        d0 = pl.multiple_of(sub_dst_ref[base + j] * SUB_ROWS, SUB_ROWS)
        return pltpu.make_async_copy(xs_ref.at[pl.ds(r0, SUB_ROWS)], xs_hbm.at[pl.ds(d0, SUB_ROWS)], sem)

    def start(j, c):
        @pl.when(sub_dst_ref[base + j] >= 0)
        def _():
            copy(j).start()
        return c

    def wait(j, c):
        @pl.when(sub_dst_ref[base + j] >= 0)
        def _():
            copy(j).wait()
        return c

    lax.fori_loop(0, n_sub, start, 0)
    lax.fori_loop(0, n_sub, wait, 0)


def _dispatch(x, pos, sub_e, sub_dst, n_sorted_rows, buf_rows):
    n, d = x.shape
    nt, e, tile = pos.shape
    zeros = jnp.zeros((n_sorted_rows, d), BF16)
    return pl.pallas_call(
        _dispatch_kernel,
        grid_spec=pltpu.PrefetchScalarGridSpec(
            num_scalar_prefetch=2,
            grid=(nt,),
            in_specs=[pl.BlockSpec((tile, d), lambda i, *_: (i, 0)),
                      pl.BlockSpec((1, e, tile), lambda i, *_: (i, 0, 0)),
                      pl.BlockSpec(memory_space=pl.ANY)],
            out_specs=pl.BlockSpec(memory_space=pl.ANY),
            scratch_shapes=[pltpu.VMEM((buf_rows, tile), BF16),
                            pltpu.VMEM((buf_rows, d), BF16),
                            pltpu.SemaphoreType.DMA(())]),
        out_shape=jax.ShapeDtypeStruct((n_sorted_rows, d), BF16),
        input_output_aliases={4: 0},
        compiler_params=_cparams(("arbitrary",)),
        name="moe_dispatch",
    )(sub_e.reshape(-1), sub_dst.reshape(-1), x, pos, zeros)


def _experts_kernel(blk_e_ref, n_used_ref, xs_ref, w1_ref, w3_ref, w2_ref, ys_ref):
    used = pl.program_id(0) < n_used_ref[0]

    @pl.when(used)
    def _():
        xb = xs_ref[...]
        h1 = _dot(xb, w1_ref[0, 0].astype(BF16))
        h3 = _dot(xb, w3_ref[0, 0].astype(BF16))
        hh = (_silu(h1) * h3).astype(BF16)
        ys_ref[...] = _dot(hh, w2_ref[0, 0].astype(BF16)).astype(ys_ref.dtype)

    @pl.when(jnp.logical_not(used))
    def _():
        ys_ref[...] = jnp.zeros_like(ys_ref)


def _experts(xs, blk_e, n_used, w1, w3, w2, layer):
    rows, d = xs.shape
    hid = w1.shape[-1]
    nblk = rows // FFN_ROWS
    return pl.pallas_call(
        _experts_kernel,
        grid_spec=pltpu.PrefetchScalarGridSpec(
            num_scalar_prefetch=2,
            grid=(nblk,),
            in_specs=[pl.BlockSpec((FFN_ROWS, d), lambda i, be, nu: (jnp.minimum(i, nu[0] - 1), 0)),
                      pl.BlockSpec((1, 1, d, hid), lambda i, be, nu: (layer, be[i], 0, 0)),
                      pl.BlockSpec((1, 1, d, hid), lambda i, be, nu: (layer, be[i], 0, 0)),
                      pl.BlockSpec((1, 1, hid, d), lambda i, be, nu: (layer, be[i], 0, 0))],
            out_specs=pl.BlockSpec((FFN_ROWS, d), lambda i, be, nu: (i, 0))),
        out_shape=jax.ShapeDtypeStruct((rows, d), BF16),
        compiler_params=_cparams(("arbitrary",)),
        name="moe_experts",
    )(blk_e, n_used, xs, w1, w3, w2)


def _combine_kernel(sub_e_ref, sub_dst_ref, x_ref, pos_ref, w_ref, ys_hbm, s1_ref, s3_ref, s2_ref,
                    lg_ref, lb_ref, o_ref, m_ref, ys_ref, sem, *, alpha):
    t = pl.program_id(0)
    tile = x_ref.shape[0]
    n_sub = m_ref.shape[0] // SUB_ROWS
    base = t * n_sub

    def copy(j):
        r0 = pl.multiple_of(j * SUB_ROWS, SUB_ROWS)
        d0 = pl.multiple_of(sub_dst_ref[base + j] * SUB_ROWS, SUB_ROWS)
        return pltpu.make_async_copy(ys_hbm.at[pl.ds(d0, SUB_ROWS)], ys_ref.at[pl.ds(r0, SUB_ROWS)], sem)

    def start(j, c):
        used = sub_dst_ref[base + j] >= 0

        @pl.when(used)
        def _():
            copy(j).start()

        @pl.when(jnp.logical_not(used))
        def _():
            r0 = pl.multiple_of(j * SUB_ROWS, SUB_ROWS)
            ys_ref[pl.ds(r0, SUB_ROWS), :] = jnp.zeros((SUB_ROWS, ys_ref.shape[1]), ys_ref.dtype)
        return c

    def build(j, c):
        r0 = pl.multiple_of(j * SUB_ROWS, SUB_ROWS)
        m_ref[pl.ds(r0, SUB_ROWS), :] = _slot_matrix(j, sub_e_ref[base + j], pos_ref, w_ref, tile)
        return c

    def wait(j, c):
        @pl.when(sub_dst_ref[base + j] >= 0)
        def _():
            copy(j).wait()
        return c

    lax.fori_loop(0, n_sub, start, 0)
    lax.fori_loop(0, n_sub, build, 0)
    lax.fori_loop(0, n_sub, wait, 0)

    x = x_ref[...]
    xb = x.astype(BF16)
    g1 = _dot(xb, s1_ref[0].astype(BF16))
    g3 = _dot(xb, s3_ref[0].astype(BF16))
    ffn = _dot((_silu(g1) * g3).astype(BF16), s2_ref[0].astype(BF16))
    tn = (((0,), (0,)), ((), ()))
    for c in range(m_ref.shape[0] // MM_ROWS):
        rs = slice(c * MM_ROWS, (c + 1) * MM_ROWS)
        ffn = ffn + lax.dot_general(m_ref[rs, :], ys_ref[rs, :], tn, preferred_element_type=F32)
    o_ref[...] = _layernorm(alpha * x + ffn, lg_ref[...], lb_ref[...])


def _combine(x, pos, wsel, ys, sub_e, sub_dst, s1, s3, s2, ln_g, ln_b, layer, alpha, buf_rows):
    n, d = x.shape
    nt, e, tile = pos.shape
    hid = s1.shape[-1]
    vec = pl.BlockSpec((1, d), lambda i, *_: (0, 0))
    return pl.pallas_call(
        functools.partial(_combine_kernel, alpha=alpha),
        grid_spec=pltpu.PrefetchScalarGridSpec(
            num_scalar_prefetch=2,
            grid=(nt,),
            in_specs=[pl.BlockSpec((tile, d), lambda i, *_: (i, 0)),
                      pl.BlockSpec((1, e, tile), lambda i, *_: (i, 0, 0)),
                      pl.BlockSpec((1, e, tile), lambda i, *_: (i, 0, 0)),
                      pl.BlockSpec(memory_space=pl.ANY),
                      pl.BlockSpec((1, d, hid), lambda i, *_: (layer, 0, 0)),
                      pl.BlockSpec((1, d, hid), lambda i, *_: (layer, 0, 0)),
                      pl.BlockSpec((1, hid, d), lambda i, *_: (layer, 0, 0)), vec, vec],
            out_specs=pl.BlockSpec((tile, d), lambda i, *_: (i, 0)),
            scratch_shapes=[pltpu.VMEM((buf_rows, tile), BF16),
                            pltpu.VMEM((buf_rows, d), BF16),
                            pltpu.SemaphoreType.DMA(())]),
        out_shape=jax.ShapeDtypeStruct((n, d), F32),
        compiler_params=_cparams(("arbitrary",)),
        name="moe_combine",
    )(sub_e.reshape(-1), sub_dst.reshape(-1), x, pos, wsel, ys, s1, s3, s2,
      ln_g.reshape(1, d), ln_b.reshape(1, d))


def _dispatch_plan(seg, buf_rows, n_sorted_rows):
    nt, ne = seg.shape
    n_sub = buf_rows // SUB_ROWS
    per_blk = FFN_ROWS // SUB_ROWS
    seg = seg.astype(jnp.int32)
    seg_end = jnp.cumsum(seg, axis=1)
    seg_start = seg_end - seg
    exp_sub = jnp.sum(seg, axis=0)
    exp_blk = (exp_sub + per_blk - 1) // per_blk
    blk_end = jnp.cumsum(exp_blk)
    exp_start = (blk_end - exp_blk) * per_blk
    dst_start = exp_start[None, :] + jnp.cumsum(seg, axis=0) - seg
    j = jnp.arange(n_sub, dtype=jnp.int32)
    sub_e = jnp.sum((seg_end[:, None, :] <= j[None, :, None]).astype(jnp.int32), axis=-1)
    used = j[None, :] < seg_end[:, -1:]
    sub_ec = jnp.minimum(sub_e, ne - 1)
    pick = sub_ec[:, :, None] == jnp.arange(ne, dtype=jnp.int32)[None, None, :]
    dst = j[None, :] + jnp.sum(jnp.where(pick, (dst_start - seg_start)[:, None, :], 0), axis=-1)
    sub_dst = jnp.where(used, dst, -1).astype(jnp.int32)
    sub_e = jnp.where(used, sub_ec, -1).astype(jnp.int32)
    nblk = n_sorted_rows // FFN_ROWS
    i = jnp.arange(nblk, dtype=jnp.int32)
    blk_e = jnp.minimum(jnp.sum((blk_end[None, :] <= i[:, None]).astype(jnp.int32), axis=-1), ne - 1)
    return sub_e, sub_dst, blk_e, blk_end[-1:].astype(jnp.int32)


def kernel(x, w_in, b_gate, hgrn_lb_logits, hgrn_norm_w, s5_a_re, s5_a_im, s5_b_re, s5_b_im,
           s5_c_re, s5_c_im, s5_d, s5_log_dt, s5_glu_w, s5_glu_b, rg_conv_w, rg_conv_b,
           rg_wa, rg_ba, rg_wx, rg_bx, rg_lambda, w_branch, w_out, ln1_g, ln1_b,
           router_w, router_bias, exp_w1, exp_w3, exp_w2, sh_w1, sh_w3, sh_w2, ln2_g, ln2_b):
    bn, s, d = x.shape
    depth = w_in.shape[0]
    n = bn * s
    w = hgrn_norm_w.shape[1]
    alpha = (2 * depth) ** 0.25

    sm = jax.nn.softmax(hgrn_lb_logits.astype(F32), axis=0)
    lower_bounds = jnp.cumsum(sm, axis=0) - sm[0:1]

    xt = x.transpose(1, 0, 2).reshape(n, d)
    seq_rows = 256 * bn
    c_hg = (0, 1, 2, 3)
    c_su = 4
    c_rg = (5, 6)
    c_gate = 7
    n_exp = router_w.shape[2]
    buf_rows = _tile_buffer_rows(MOE_TILE, n_exp, TOP_K)
    per_blk = FFN_ROWS // SUB_ROWS
    max_sub = n * TOP_K // SUB_ROWS + (n // MOE_TILE) * n_exp + n_exp * (per_blk - 1)
    n_sorted = -(-max_sub // per_blk) * FFN_ROWS
    for l in range(depth):
        proj = _matmul(xt, w_in, l, 1024, w)
        ya = _hgrn2(proj, c_hg, lower_bounds[l], hgrn_norm_w[l], bn, seq_rows)
        mats = _s5_matrices(s5_a_re[l], s5_a_im[l], s5_b_re[l], s5_b_im[l],
                            s5_c_re[l], s5_c_im[l], s5_log_dt[l])
        yb = _s5(proj, c_su, mats, s5_d[l], s5_glu_w[l], s5_glu_b[l], bn, seq_rows // 2)
        yc = _rglru(proj, c_rg, rg_conv_w[l], rg_conv_b[l], rg_wa[l], rg_ba[l], rg_wx[l], rg_bx[l],
                    jax.nn.softplus(-rg_lambda[l].astype(F32)), bn, seq_rows)
        x1 = _merge(xt, ya, yb, yc, proj, c_gate, w_branch[l], b_gate[l], w_out[l],
                    ln1_g[l], ln1_b[l], alpha, 512)
        pos, wsel, seg = _router(x1, router_w[l].T, router_bias[l], MOE_TILE)
        sub_e, sub_dst, blk_e, n_used = _dispatch_plan(seg[:, :, 0], buf_rows, n_sorted)
        xs = _dispatch(x1, pos, sub_e, sub_dst, n_sorted, buf_rows)
        ys = _experts(xs, blk_e, n_used, exp_w1, exp_w3, exp_w2, l)
        xt = _combine(x1, pos, wsel, ys, sub_e, sub_dst, sh_w1, sh_w3, sh_w2, ln2_g[l], ln2_b[l],
                      l, alpha, buf_rows)
    return xt.reshape(s, bn, d).transpose(1, 0, 2)
```

```python
import functools
import math

import numpy as np
import jax
import jax.numpy as jnp
from jax import lax
from jax.experimental import pallas as pl
from jax.experimental.pallas import tpu as pltpu

F32 = jnp.float32
BF16 = jnp.bfloat16

HG_HEADS = 4
HG_HEAD_DIM = 128
HG_CHUNK = 16
S5_GROUP_CH = 16
S5_STATE = 64
RG_BLOCK_W = 128
RG_CONV = 4
RG_C = 8.0
N_EXPERTS = 64
N_EXPERT_GROUPS = 8
TOPK_GROUPS = 4
TOP_K = 8
ROUTED_SCALE = 2.5
LN_EPS = 1e-5
RMS_EPS = 1e-6
LANES = 128
SUBLANES = 8
VMEM_LIMIT = 56 * 1024 * 1024
SUB_ROWS = 16
MOE_TILE = 512
FFN_ROWS = 512
MM_ROWS = 512


def _cparams(sem):
    return pltpu.CompilerParams(dimension_semantics=sem, vmem_limit_bytes=VMEM_LIMIT)


def _sigmoid(x):
    return 1.0 / (1.0 + jnp.exp(-x))


def _silu(x):
    return x * _sigmoid(x)


def _gelu_tanh(x):
    c = math.sqrt(2.0 / math.pi)
    return 0.5 * x * (1.0 + jnp.tanh(c * (x + 0.044715 * (x * x * x))))


def _layernorm(z, g, b):
    mu = jnp.mean(z, axis=-1, keepdims=True)
    zc = z - mu
    var = jnp.mean(zc * zc, axis=-1, keepdims=True)
    return zc * lax.rsqrt(var + LN_EPS) * g + b


def _dot(a, b):
    return jnp.dot(a, b, preferred_element_type=F32)


def _rows(ref):
    return ref[0] if len(ref.shape) == 3 else ref[...]


def _token_spec(x, batch, rows, index):
    if x.ndim == 3:
        def imap3(*g):
            i, b = index(*g)
            return (b, i, 0)
        return pl.BlockSpec((1, rows, x.shape[2]), imap3)
    return pl.BlockSpec((rows, x.shape[1] // batch), lambda *g: index(*g))


def _matmul_kernel(x_ref, w_ref, o_ref, xb_ref):
    @pl.when(pl.program_id(2) == 0)
    def _():
        xb_ref[...] = _rows(x_ref).astype(BF16)

    o_ref[...] = _dot(xb_ref[...], w_ref[0].astype(BF16)).astype(o_ref.dtype)


def _in_proj(x, w, layer, batch, tm, tn):
    s = x.shape[1] if x.ndim == 3 else x.shape[0]
    k, n = w.shape[1], w.shape[2]
    nj = n // tn
    out = pl.pallas_call(
        _matmul_kernel,
        grid=(batch, s // tm, nj),
        in_specs=[_token_spec(x, batch, tm, lambda b, i, j: (i, b)),
                  pl.BlockSpec((1, k, tn), lambda b, i, j: (layer, 0, j))],
        out_specs=pl.BlockSpec((tm, tn), lambda b, i, j: (i, b * nj + j)),
        out_shape=jax.ShapeDtypeStruct((s, batch * n), BF16),
        scratch_shapes=[pltpu.VMEM((tm, k), BF16)],
        compiler_params=_cparams(("parallel", "parallel", "arbitrary")),
        name="in_proj",
    )(x, w)
    return out.reshape(s * batch, n)


def _hgrn2_kernel(q_ref, f_ref, v_ref, g_ref, lb_ref, nw_ref, o_ref, st_ref, *, batch):
    rows = q_ref.shape[0]
    cr = HG_CHUNK * batch
    n_chunks = rows // cr
    dh = HG_HEAD_DIM

    @pl.when(pl.program_id(0) == 0)
    def _():
        st_ref[...] = jnp.zeros_like(st_ref)

    row = lax.broadcasted_iota(jnp.int32, (cr, dh), 0)
    row_b = row % batch
    ones_sum = jnp.ones((dh, dh), BF16)

    def chunk(c, carry):
        r0 = pl.multiple_of(c * cr, cr)
        for h in range(HG_HEADS):
            ls = slice(h * dh, (h + 1) * dh)
            lb = lb_ref[:, ls]
            f = lb + (1.0 - lb) * _sigmoid(f_ref[pl.ds(r0, cr), ls].astype(F32))
            q = _silu(q_ref[pl.ds(r0, cr), ls].astype(F32))
            k = 1.0 - f
            v = v_ref[pl.ds(r0, cr), ls].astype(F32)
            bc = jnp.log(f)
            sh = batch
            while sh < cr:
                bc = bc + jnp.where(row >= sh, pltpu.roll(bc, sh, 0), 0.0)
                sh *= 2
            p0 = (q * k).astype(BF16)
            o = _dot(p0, ones_sum) * v
            for j in range(1, HG_CHUNK):
                s = j * batch
                dec = jnp.exp(bc - pltpu.roll(bc, s, 0))
                p = jnp.where(row >= s, q * pltpu.roll(k, s, 0) * dec, 0.0)
                o = o + _dot(p.astype(BF16), ones_sum) * pltpu.roll(v, s, 0)
            b_last = bc[cr - batch:cr, :]
            qt = q * jnp.exp(bc)
            kt = k * jnp.exp(jnp.concatenate([b_last] * HG_CHUNK, axis=0) - bc)
            qm = jnp.concatenate([jnp.where(row_b == b, qt, 0.0) for b in range(batch)], axis=1)
            km = jnp.concatenate([jnp.where(row_b == b, kt, 0.0) for b in range(batch)], axis=1)
            st = st_ref[h]
            o = o + lax.dot_general(qm.astype(BF16), st.astype(BF16),
                                    (((1,), (1,)), ((), ())), preferred_element_type=F32)
            kv = lax.dot_general(v.astype(BF16), km.astype(BF16),
                                 (((0,), (0,)), ((), ())), preferred_element_type=F32)
            dec_all = jnp.concatenate([jnp.exp(b_last[b:b + 1, :]) for b in range(batch)], axis=1)
            st_ref[h] = st * dec_all + kv
            o = o * lax.rsqrt(jnp.mean(o * o, axis=-1, keepdims=True) + RMS_EPS)
            o_ref[pl.ds(r0, cr), ls] = o * nw_ref[:, ls] * _silu(g_ref[pl.ds(r0, cr), ls].astype(F32))
        return carry

    lax.fori_loop(0, n_chunks, chunk, 0)


def _hgrn2(proj, cols, lb, norm_w, batch, rows):
    n = proj.shape[0]
    w = HG_HEADS * HG_HEAD_DIM
    cq, cf, cv, cg = cols

    def spec(cb):
        return pl.BlockSpec((rows, w), lambda i: (i, cb))

    vec = pl.BlockSpec((1, w), lambda i: (0, 0))
    return pl.pallas_call(
        functools.partial(_hgrn2_kernel, batch=batch),
        grid=(n // rows,),
        in_specs=[spec(cq), spec(cf), spec(cv), spec(cg), vec, vec],
        out_specs=pl.BlockSpec((rows, w), lambda i: (i, 0)),
        out_shape=jax.ShapeDtypeStruct((n, w), F32),
        scratch_shapes=[pltpu.VMEM((HG_HEADS, HG_HEAD_DIM, batch * HG_HEAD_DIM), F32)],
        compiler_params=_cparams(("arbitrary",)),
        name="hgrn2",
    )(proj, proj, proj, proj, lb.reshape(1, w), norm_w.reshape(1, w))


S5_TILE = 128


def _s5_kernel(u_ref, pin_ref, bm_ref, ar_ref, ai_ref, qout_ref, cm_ref, d_ref, gw_ref, gb_ref,
               o_ref, hs_ref, h_ref, *, batch):
    rows, w = u_ref.shape
    nblk = w // LANES
    nc = hs_ref.shape[1]
    cpb = nc // nblk
    tile2 = 2 * S5_TILE
    n_tiles = rows // S5_TILE
    steps = rows // batch

    @pl.when(pl.program_id(0) == 0)
    def _():
        h_ref[...] = jnp.zeros_like(h_ref)

    for t in range(n_tiles):
        ub = u_ref[t * S5_TILE:(t + 1) * S5_TILE, :].astype(BF16)
        up = _dot(pin_ref[...], ub).astype(BF16)
        for j in range(nblk):
            ls = slice(j * LANES, (j + 1) * LANES)
            lhs = jnp.concatenate([up[0:tile2, ls], up[tile2:2 * tile2, ls]], axis=1)
            hs_ref[t * tile2:(t + 1) * tile2, j * cpb:(j + 1) * cpb] = _dot(lhs, bm_ref[j])

    top = lax.broadcasted_iota(jnp.int32, (2 * batch, LANES), 0) < batch
    groups = nc // LANES
    per_pass = 8
    unroll = 4
    for p in range(groups // per_pass):
        cols = [(p * per_pass + g) * LANES for g in range(per_pass)]
        ar8 = [jnp.broadcast_to(ar_ref[:, c:c + LANES], (2 * batch, LANES)) for c in cols]
        ai8 = [jnp.where(top, -1.0, 1.0) * ai_ref[:, c:c + LANES] for c in cols]

        def body(i, hs):
            hs = list(hs)
            for s in range(unroll):
                r0 = pl.multiple_of((i * unroll + s) * (2 * batch), 2 * batch)
                for g, c in enumerate(cols):
                    h = (ar8[g] * hs[g] + ai8[g] * pltpu.roll(hs[g], batch, 0)
                         + hs_ref[pl.ds(r0, 2 * batch), c:c + LANES])
                    hs_ref[pl.ds(r0, 2 * batch), c:c + LANES] = h
                    hs[g] = h
            return tuple(hs)

        h0 = tuple(h_ref[:, c:c + LANES] for c in cols)
        hn = lax.fori_loop(0, steps // unroll, body, h0)
        for g, c in enumerate(cols):
            h_ref[:, c:c + LANES] = hn[g]

    for t in range(n_tiles):
        hb = hs_ref[t * tile2:(t + 1) * tile2, :].astype(BF16)
        hre = _dot(qout_ref[0], hb).astype(BF16)
        him = _dot(qout_ref[1], hb).astype(BF16)
        ys = []
        for j in range(nblk):
            cs = slice(j * cpb, (j + 1) * cpb)
            ys.append(_dot(jnp.concatenate([hre[:, cs], him[:, cs]], axis=1), cm_ref[j]))
        rs = slice(t * S5_TILE, (t + 1) * S5_TILE)
        y = jnp.concatenate(ys, axis=1) + d_ref[...] * u_ref[rs, :].astype(F32)
        y = _gelu_tanh(y)
        z = _dot(y.astype(BF16), gw_ref[...].astype(BF16)) + gb_ref[...]
        o_ref[rs, :] = y * _sigmoid(z)


def _s5_layout_matrices(batch):
    assert 2 * batch == SUBLANES, "one time step must fill the 8 sublanes"
    tile2 = 2 * S5_TILE
    r2 = np.arange(tile2)
    src = (r2 // (2 * batch)) * batch + r2 % batch
    is_re = (r2 % (2 * batch)) < batch
    onehot = (src[:, None] == np.arange(S5_TILE)[None, :])
    p_re = onehot & is_re[:, None]
    p_im = onehot & ~is_re[:, None]
    pin = np.concatenate([p_re, p_im], axis=0).astype(np.float32)
    qout = np.stack([p_re.T, p_im.T], axis=0).astype(np.float32)
    return jnp.asarray(pin, BF16), jnp.asarray(qout, BF16)


def _s5(proj, col, mats, d, glu_w, glu_b, batch, rows):
    n = proj.shape[0]
    w = d.shape[0]
    bmat, ar, ai, cmat = mats
    nblk, _, cpb = bmat.shape
    nc = nblk * cpb
    pin, qout = _s5_layout_matrices(batch)
    full = lambda a: pl.BlockSpec(a.shape, lambda i, nd=a.ndim: (0,) * nd)
    return pl.pallas_call(
        functools.partial(_s5_kernel, batch=batch),
        grid=(n // rows,),
        in_specs=[pl.BlockSpec((rows, w), lambda i: (i, col)),
                  full(pin), full(bmat), full(ar), full(ai), full(qout), full(cmat),
                  pl.BlockSpec((1, w), lambda i: (0, 0)),
                  pl.BlockSpec((w, w), lambda i: (0, 0)),
                  pl.BlockSpec((1, w), lambda i: (0, 0))],
        out_specs=pl.BlockSpec((rows, w), lambda i: (i, 0)),
        out_shape=jax.ShapeDtypeStruct((n, w), F32),
        scratch_shapes=[pltpu.VMEM((2 * rows, nc), F32),
                        pltpu.VMEM((2 * batch, nc), F32)],
        compiler_params=_cparams(("arbitrary",)),
        name="s5",
    )(proj, pin, bmat, ar, ai, qout, cmat, d.reshape(1, w), glu_w, glu_b.reshape(1, w))


def _s5_matrices(a_re, a_im, b_re, b_im, c_re, c_im, log_dt):
    g, p = a_re.shape
    ch = b_re.shape[-1]
    gpb = LANES // ch
    nblk = g // gpb
    dt = jnp.exp(log_dt.astype(F32))[:, None]
    mag = jnp.exp(a_re * dt)
    abr = mag * jnp.cos(a_im * dt)
    abi = mag * jnp.sin(a_im * dt)
    den = a_re * a_re + a_im * a_im
    cr = ((abr - 1.0) * a_re + abi * a_im) / den
    ci = (abi * a_re - (abr - 1.0) * a_im) / den
    bbr = cr[..., None] * b_re - ci[..., None] * b_im
    bbi = cr[..., None] * b_im + ci[..., None] * b_re
    eye = jnp.eye(gpb, dtype=F32)

    def in_block(m):
        m = m.reshape(nblk, gpb, p, ch)
        return jnp.einsum('jgpc,gh->jgchp', m, eye).reshape(nblk, gpb * ch, gpb * p)

    def out_block(m):
        m = m.reshape(nblk, gpb, ch, p)
        return jnp.einsum('jgcp,gh->jgphc', m, eye).reshape(nblk, gpb * p, gpb * ch)

    bmat = jnp.concatenate([in_block(bbr), in_block(bbi)], axis=1).astype(BF16)
    cmat = jnp.concatenate([out_block(c_re), -out_block(c_im)], axis=1).astype(BF16)
    return bmat, abr.reshape(1, g * p), abi.reshape(1, g * p), cmat


def _rglru_kernel(xg_ref, xr_ref, cw_ref, cb_ref, wa_ref, ba_ref, wx_ref, bx_ref, sp_ref,
                  o_ref, xp_ref, a_ref, b_ref, h_ref, *, batch):
    rows, w = xr_ref.shape
    halo = SUBLANES * ((RG_CONV - 1) * batch // SUBLANES + 1)
    pair = 2 * batch

    @pl.when(pl.program_id(0) == 0)
    def _():
        xp_ref[0:halo, :] = jnp.zeros((halo, w), F32)
        h_ref[...] = jnp.zeros_like(h_ref)

    xr = xr_ref[...].astype(F32)
    xp_ref[halo:halo + rows, :] = xr
    xc = cb_ref[...] + cw_ref[RG_CONV - 1:RG_CONV, :] * xr
    for i in range(RG_CONV - 1):
        back = (RG_CONV - 1 - i) * batch
        xc = xc + cw_ref[i:i + 1, :] * xp_ref[halo - back:halo - back + rows, :]
    xp_ref[0:halo, :] = xr[rows - halo:rows, :]

    xcb = xc.astype(BF16)
    nb = w // RG_BLOCK_W
    ra, ri = [], []
    for hblk in range(nb):
        xs = xcb[:, hblk * RG_BLOCK_W:(hblk + 1) * RG_BLOCK_W]
        ra.append(_dot(xs, wa_ref[hblk].astype(BF16)))
        ri.append(_dot(xs, wx_ref[hblk].astype(BF16)))
    r = _sigmoid(jnp.concatenate(ra, axis=1) + ba_ref[...])
    ig = _sigmoid(jnp.concatenate(ri, axis=1) + bx_ref[...])
    log_a = (-RG_C) * r * sp_ref[...]
    a = jnp.exp(log_a)
    a_ref[...] = a
    b_ref[...] = jnp.sqrt(-jnp.tanh(log_a) * (a * a + 1.0)) * (ig * xc)

    top = lax.broadcasted_iota(jnp.int32, (pair, w), 0) < batch
    unroll = 4

    def body(i, h):
        for s in range(unroll):
            r0 = pl.multiple_of((i * unroll + s) * pair, pair)
            at = a_ref[pl.ds(r0, pair), :]
            bt = b_ref[pl.ds(r0, pair), :]
            n1 = at * h + bt
            h1 = jnp.where(top, n1, pltpu.roll(n1, batch, 0))
            n2 = at * h1 + bt
            b_ref[pl.ds(r0, pair), :] = jnp.where(top, n1, n2)
            h = jnp.where(top, pltpu.roll(n2, batch, 0), n2)
        return h

    h_ref[...] = lax.fori_loop(0, rows // (pair * unroll), body, h_ref[...])
    o_ref[...] = b_ref[...] * _gelu_tanh(xg_ref[...].astype(F32))


def _rglru(proj, cols, conv_w, conv_b, wa, ba, wx, bx, softplus_neg_lam, batch, rows):
    assert 2 * batch == SUBLANES, "two time steps must fill the 8 sublanes"
    n = proj.shape[0]
    w = conv_b.shape[0]
    cg, cx = cols
    nb = wa.shape[0]
    halo = SUBLANES * ((RG_CONV - 1) * batch // SUBLANES + 1)
    vec = pl.BlockSpec((1, w), lambda i: (0, 0))
    blk = pl.BlockSpec((nb, RG_BLOCK_W, RG_BLOCK_W), lambda i: (0, 0, 0))
    return pl.pallas_call(
        functools.partial(_rglru_kernel, batch=batch),
        grid=(n // rows,),
        in_specs=[pl.BlockSpec((rows, w), lambda i: (i, cg)),
                  pl.BlockSpec((rows, w), lambda i: (i, cx)),
                  pl.BlockSpec((RG_CONV, w), lambda i: (0, 0)),
                  vec, blk, vec, blk, vec, vec],
        out_specs=pl.BlockSpec((rows, w), lambda i: (i, 0)),
        out_shape=jax.ShapeDtypeStruct((n, w), F32),
        scratch_shapes=[pltpu.VMEM((halo + rows, w), F32),
                        pltpu.VMEM((rows, w), F32),
                        pltpu.VMEM((rows, w), F32),
                        pltpu.VMEM((2 * batch, w), F32)],
        compiler_params=_cparams(("arbitrary",)),
        name="rglru",
    )(proj, proj, conv_w, conv_b.reshape(1, w), wa, ba.reshape(1, w), wx, bx.reshape(1, w),
      softplus_neg_lam.reshape(1, w))


def _merge_kernel(x_ref, ya_ref, yb_ref, yc_ref, *rest, alpha, halves):
    n_gate = 3 * halves
    gate_refs = rest[:n_gate]
    wb_ref, bg_ref, wo_ref, lg_ref, lb_ref, o_ref = rest[n_gate:]
    m = None
    for kbr, y_ref in enumerate((ya_ref, yb_ref, yc_ref)):
        br = _dot(y_ref[...].astype(BF16), wb_ref[0, kbr].astype(BF16))
        gp = jnp.concatenate([gate_refs[kbr * halves + i][...] for i in range(halves)], axis=1)
        t = _sigmoid(gp.astype(F32) + bg_ref[kbr:kbr + 1, :]) * br
        m = t if m is None else m + t
    mix = _dot(m.astype(BF16), wo_ref[0].astype(BF16))
    o_ref[...] = _layernorm(alpha * _rows(x_ref) + mix, lg_ref[...], lb_ref[...])


def _merge(x, ya, yb, yc, proj, gate_col0, w_branch, b_gate, w_out, ln_g, ln_b, layer, alpha, batch, tm):
    s = x.shape[1] if x.ndim == 3 else x.shape[0]
    d = w_out.shape[-1]
    w = ya.shape[1]
    nbr = w_branch.shape[1]
    halves = d // w
    pcols = proj.shape[1] // w
    view = lambda a: a.reshape(s, batch * a.shape[1])
    ysp = pl.BlockSpec((tm, w), lambda i, b: (i, b))
    gates = [pl.BlockSpec((tm, w), lambda i, b, c=gate_col0 + c: (i, b * pcols + c))
             for c in range(nbr * halves)]
    vec = pl.BlockSpec((1, d), lambda i, b: (0, 0))
    return pl.pallas_call(
        functools.partial(_merge_kernel, alpha=alpha, halves=halves),
        grid=(s // tm, batch),
        in_specs=[_token_spec(x, batch, tm, lambda i, b: (i, b)), ysp, ysp, ysp] + gates + [
            pl.BlockSpec((1, nbr, w, d), lambda i, b: (layer, 0, 0, 0)),
            pl.BlockSpec((nbr, d), lambda i, b: (0, 0)),
            pl.BlockSpec((1, d, d), lambda i, b: (layer, 0, 0)), vec, vec],
        out_specs=pl.BlockSpec((tm, d), lambda i, b: (i, b)),
        out_shape=jax.ShapeDtypeStruct((s, batch * d), F32),
        compiler_params=_cparams(("parallel", "parallel")),
        name="merge",
    )(x, view(ya), view(yb), view(yc), *([view(proj)] * (nbr * halves)), w_branch,
      b_gate.reshape(nbr, d), w_out, ln_g.reshape(1, d), ln_b.reshape(1, d))


def _first_max_mask(cur, idx, axis):
    m = jnp.max(cur, axis=axis, keepdims=True)
    first = jnp.min(jnp.where(cur == m, idx, jnp.int32(2 ** 30)), axis=axis, keepdims=True)
    return idx == first


def _router_kernel(x_ref, wr_ref, rb_ref, pos_ref, w_ref, seg_ref):
    tm = x_ref.shape[0]
    e = N_EXPERTS
    per = e // N_EXPERT_GROUPS
    x = x_ref[...]
    wr = wr_ref[...]
    xh = x.astype(BF16)
    xl = (x - xh.astype(F32)).astype(BF16)
    wh = wr.astype(BF16)
    wl = (wr - wh.astype(F32)).astype(BF16)
    nt = (((1,), (1,)), ((), ()))
    logits = (lax.dot_general(wh, xh, nt, preferred_element_type=F32)
              + lax.dot_general(wh, xl, nt, preferred_element_type=F32)
              + lax.dot_general(wl, xh, nt, preferred_element_type=F32))
    scores = _sigmoid(logits)
    biased = scores + rb_ref[...]
    neg = jnp.float32(-jnp.inf)

    b3 = biased.reshape(N_EXPERT_GROUPS, per, tm)
    i3 = lax.broadcasted_iota(jnp.int32, b3.shape, 1)
    top1 = _first_max_mask(b3, i3, 1)
    m1 = jnp.max(b3, axis=1, keepdims=True)
    m2 = jnp.max(jnp.where(top1, neg, b3), axis=1, keepdims=True)
    gscore = (m1 + m2).reshape(N_EXPERT_GROUPS, tm)

    ig = lax.broadcasted_iota(jnp.int32, gscore.shape, 0)
    gsel = jnp.zeros(gscore.shape, F32)
    cur = gscore
    for _ in range(TOPK_GROUPS):
        pick = _first_max_mask(cur, ig, 0)
        gsel = jnp.where(pick, 1.0, gsel)
        cur = jnp.where(pick, neg, cur)

    gsel3 = jnp.broadcast_to(gsel.reshape(N_EXPERT_GROUPS, 1, tm), b3.shape)
    masked = jnp.where(gsel3 > 0.0, b3, neg).reshape(e, tm)
    ie = lax.broadcasted_iota(jnp.int32, masked.shape, 0)
    chosen = jnp.zeros(masked.shape, F32)
    cur = masked
    for _ in range(TOP_K):
        pick = _first_max_mask(cur, ie, 0)
        chosen = jnp.where(pick, 1.0, chosen)
        cur = jnp.where(pick, neg, cur)

    wsel = jnp.where(chosen > 0.0, scores, 0.0)
    wsel = wsel / jnp.sum(wsel, axis=0, keepdims=True) * ROUTED_SCALE
    w_ref[0] = wsel

    cb = chosen.astype(BF16)
    tok_r = lax.broadcasted_iota(jnp.int32, (tm, tm), 0)
    tok_c = lax.broadcasted_iota(jnp.int32, (tm, tm), 1)
    before = jnp.where(tok_r < tok_c, 1.0, 0.0).astype(BF16)
    prefix = _dot(cb, before)
    cnt = jnp.sum(chosen, axis=1, keepdims=True)
    seg = jnp.floor((cnt + (SUB_ROWS - 1)) * (1.0 / SUB_ROWS))
    seg_b = jnp.broadcast_to(seg, (e, LANES))
    ex_r = lax.broadcasted_iota(jnp.int32, (e, e), 0)
    ex_c = lax.broadcasted_iota(jnp.int32, (e, e), 1)
    earlier = jnp.where(ex_c < ex_r, 1.0, 0.0).astype(BF16)
    seg_off = _dot(earlier, seg_b.astype(BF16))
    seg_ref[0] = seg_b
    pos_ref[0] = jnp.where(chosen > 0.0, seg_off[:, 0:1] * SUB_ROWS + prefix, -1.0)


def _tile_index(batch):
    return lambda t, *_: (t // batch, t % batch)


def _router(x, router_w_t, router_bias, batch, tm):
    s = x.shape[0]
    e, d = router_w_t.shape
    nt = (s // tm) * batch
    tile = lambda width: pl.BlockSpec((1, e, width), lambda i: (i, 0, 0))
    return pl.pallas_call(
        _router_kernel,
        grid=(nt,),
        in_specs=[_token_spec(x, batch, tm, _tile_index(batch)),
                  pl.BlockSpec((e, d), lambda i: (0, 0)),
                  pl.BlockSpec((e, 1), lambda i: (0, 0))],
        out_specs=[tile(tm), tile(tm), tile(LANES)],
        out_shape=[jax.ShapeDtypeStruct((nt, e, tm), F32),
                   jax.ShapeDtypeStruct((nt, e, tm), F32),
                   jax.ShapeDtypeStruct((nt, e, LANES), F32)],
        compiler_params=_cparams(("parallel",)),
        name="router",
    )(x, router_w_t, router_bias.reshape(e, 1))


def _tile_buffer_rows(tile, n_experts, top_k):
    rows = tile * top_k + n_experts * (SUB_ROWS - 1)
    return -(-rows // MM_ROWS) * MM_ROWS


def _slot_matrix(j, e, pos_ref, val_ref, tile):
    e = jnp.maximum(e, 0)
    prow = pos_ref[0, pl.ds(e, 1), :]
    tgt = (lax.broadcasted_iota(jnp.int32, (SUB_ROWS, tile), 0) + j * SUB_ROWS).astype(F32)
    if val_ref is None:
        return jnp.where(prow == tgt, 1.0, 0.0).astype(BF16)
    return jnp.where(prow == tgt, val_ref[0, pl.ds(e, 1), :], 0.0).astype(BF16)


LOOP_UNROLL = 8


def _wait_sub_blocks(src, dst, sem, n_sub):
    def wait(i, c):
        for _ in range(LOOP_UNROLL):
            pltpu.make_async_copy(src.at[pl.ds(0, SUB_ROWS)], dst.at[pl.ds(0, SUB_ROWS)], sem).wait()
        return c

    lax.fori_loop(0, n_sub // LOOP_UNROLL, wait, 0)


def _dispatch_kernel(sub_e_ref, sub_dst_ref, x_ref, pos_ref, init_ref, xs_hbm, m_ref, xs_ref, sem):
    del init_ref
    t = pl.program_id(0)
    tile = _rows(x_ref).shape[0]
    n_sub = m_ref.shape[0] // SUB_ROWS
    base = t * n_sub

    def build(j, c):
        r0 = pl.multiple_of(j * SUB_ROWS, SUB_ROWS)
        m_ref[pl.ds(r0, SUB_ROWS), :] = _slot_matrix(j, sub_e_ref[base + j], pos_ref, None, tile)
        return c

    lax.fori_loop(0, n_sub, build, 0, unroll=LOOP_UNROLL)
    xb = _rows(x_ref).astype(BF16)
    for c in range(m_ref.shape[0] // MM_ROWS):
        rs = slice(c * MM_ROWS, (c + 1) * MM_ROWS)
        xs_ref[rs, :] = _dot(m_ref[rs, :], xb).astype(BF16)

    def start(j, c):
        r0 = pl.multiple_of(j * SUB_ROWS, SUB_ROWS)
        d0 = pl.multiple_of(sub_dst_ref[base + j] * SUB_ROWS, SUB_ROWS)
        pltpu.make_async_copy(xs_ref.at[pl.ds(r0, SUB_ROWS)], xs_hbm.at[pl.ds(d0, SUB_ROWS)], sem).start()
        return c

    lax.fori_loop(0, n_sub, start, 0, unroll=LOOP_UNROLL)
    _wait_sub_blocks(xs_ref, xs_hbm, sem, n_sub)


def _dispatch(x, pos, init, sub_e, sub_dst, batch, buf_rows):
    nt, e, tile = pos.shape
    d = init.shape[1]
    return pl.pallas_call(
        _dispatch_kernel,
        grid_spec=pltpu.PrefetchScalarGridSpec(
            num_scalar_prefetch=2,
            grid=(nt,),
            in_specs=[_token_spec(x, batch, tile, _tile_index(batch)),
                      pl.BlockSpec((1, e, tile), lambda i, *_: (i, 0, 0)),
                      pl.BlockSpec(memory_space=pl.ANY)],
            out_specs=pl.BlockSpec(memory_space=pl.ANY),
            scratch_shapes=[pltpu.VMEM((buf_rows, tile), BF16),
                            pltpu.VMEM((buf_rows, d), BF16),
                            pltpu.SemaphoreType.DMA(())]),
        out_shape=jax.ShapeDtypeStruct(init.shape, BF16),
        input_output_aliases={4: 0},
        compiler_params=_cparams(("arbitrary",)),
        name="moe_dispatch",
    )(sub_e.reshape(-1), sub_dst.reshape(-1), x, pos, init)


def _experts_kernel(blk_e_ref, n_used_ref, xs_ref, w1_ref, w3_ref, w2_ref, ys_ref, w13_ref, w2b_ref):
    i = pl.program_id(0)
    hid = w1_ref.shape[-1]
    used = i < n_used_ref[0]
    new_expert = jnp.logical_or(i == 0, blk_e_ref[i] != blk_e_ref[jnp.maximum(i - 1, 0)])

    @pl.when(jnp.logical_and(used, new_expert))
    def _():
        w13_ref[:, 0:hid] = w1_ref[0, 0].astype(BF16)
        w13_ref[:, hid:2 * hid] = w3_ref[0, 0].astype(BF16)
        w2b_ref[...] = w2_ref[0, 0].astype(BF16)

    @pl.when(used)
    def _():
        h = _dot(xs_ref[...], w13_ref[...])
        hh = (_silu(h[:, 0:hid]) * h[:, hid:2 * hid]).astype(BF16)
        ys_ref[...] = _dot(hh, w2b_ref[...]).astype(ys_ref.dtype)

    @pl.when(jnp.logical_not(used))
    def _():
        ys_ref[...] = jnp.zeros_like(ys_ref)


def _experts(xs, blk_e, n_used, w1, w3, w2, layer):
    rows, d = xs.shape
    hid = w1.shape[-1]
    nblk = rows // FFN_ROWS
    return pl.pallas_call(
        _experts_kernel,
        grid_spec=pltpu.PrefetchScalarGridSpec(
            num_scalar_prefetch=2,
            grid=(nblk,),
            in_specs=[pl.BlockSpec((FFN_ROWS, d), lambda i, be, nu: (jnp.minimum(i, nu[0] - 1), 0)),
                      pl.BlockSpec((1, 1, d, hid), lambda i, be, nu: (layer, be[i], 0, 0)),
                      pl.BlockSpec((1, 1, d, hid), lambda i, be, nu: (layer, be[i], 0, 0)),
                      pl.BlockSpec((1, 1, hid, d), lambda i, be, nu: (layer, be[i], 0, 0))],
            out_specs=pl.BlockSpec((FFN_ROWS, d), lambda i, be, nu: (i, 0)),
            scratch_shapes=[pltpu.VMEM((d, 2 * hid), BF16), pltpu.VMEM((hid, d), BF16)]),
        out_shape=jax.ShapeDtypeStruct((rows, d), BF16),
        compiler_params=_cparams(("arbitrary",)),
        name="moe_experts",
    )(blk_e, n_used, xs, w1, w3, w2)


def _combine_kernel(sub_e_ref, sub_dst_ref, x_ref, pos_ref, w_ref, ys_hbm, s1_ref, s3_ref, s2_ref,
                    lg_ref, lb_ref, o_ref, m_ref, ys_ref, sem, *, alpha):
    t = pl.program_id(0)
    tile = _rows(x_ref).shape[0]
    n_sub = m_ref.shape[0] // SUB_ROWS
    base = t * n_sub

    def start(j, c):
        r0 = pl.multiple_of(j * SUB_ROWS, SUB_ROWS)
        d0 = pl.multiple_of(sub_dst_ref[base + j] * SUB_ROWS, SUB_ROWS)
        pltpu.make_async_copy(ys_hbm.at[pl.ds(d0, SUB_ROWS)], ys_ref.at[pl.ds(r0, SUB_ROWS)], sem).start()
        return c

    def build(j, c):
        r0 = pl.multiple_of(j * SUB_ROWS, SUB_ROWS)
        m_ref[pl.ds(r0, SUB_ROWS), :] = _slot_matrix(j, sub_e_ref[base + j], pos_ref, w_ref, tile)
        return c

    lax.fori_loop(0, n_sub, start, 0, unroll=LOOP_UNROLL)
    lax.fori_loop(0, n_sub, build, 0, unroll=LOOP_UNROLL)
    _wait_sub_blocks(ys_hbm, ys_ref, sem, n_sub)

    x = _rows(x_ref)
    xb = x.astype(BF16)
    g1 = _dot(xb, s1_ref[0].astype(BF16))
    g3 = _dot(xb, s3_ref[0].astype(BF16))
    ffn = _dot((_silu(g1) * g3).astype(BF16), s2_ref[0].astype(BF16))
    tn = (((0,), (0,)), ((), ()))
    for c in range(m_ref.shape[0] // MM_ROWS):
        rs = slice(c * MM_ROWS, (c + 1) * MM_ROWS)
        ffn = ffn + lax.dot_general(m_ref[rs, :], ys_ref[rs, :], tn, preferred_element_type=F32)
    y = _layernorm(alpha * x + ffn, lg_ref[...], lb_ref[...])
    o_ref[...] = y.reshape(o_ref.shape)


def _combine(x, pos, wsel, ys, sub_e, sub_dst, s1, s3, s2, ln_g, ln_b, layer, alpha, batch, buf_rows,
             batch_major_out):
    s = x.shape[0]
    nt, e, tile = pos.shape
    d = ys.shape[1]
    hid = s1.shape[-1]
    vec = pl.BlockSpec((1, d), lambda i, *_: (0, 0))
    out = jax.ShapeDtypeStruct((batch, s, d) if batch_major_out else (s, batch * d), F32)
    return pl.pallas_call(
        functools.partial(_combine_kernel, alpha=alpha),
        grid_spec=pltpu.PrefetchScalarGridSpec(
            num_scalar_prefetch=2,
            grid=(nt,),
            in_specs=[_token_spec(x, batch, tile, _tile_index(batch)),
                      pl.BlockSpec((1, e, tile), lambda i, *_: (i, 0, 0)),
                      pl.BlockSpec((1, e, tile), lambda i, *_: (i, 0, 0)),
                      pl.BlockSpec(memory_space=pl.ANY),
                      pl.BlockSpec((1, d, hid), lambda i, *_: (layer, 0, 0)),
                      pl.BlockSpec((1, d, hid), lambda i, *_: (layer, 0, 0)),
                      pl.BlockSpec((1, hid, d), lambda i, *_: (layer, 0, 0)), vec, vec],
            out_specs=_token_spec(out, batch, tile, _tile_index(batch)),
            scratch_shapes=[pltpu.VMEM((buf_rows, tile), BF16),
                            pltpu.VMEM((buf_rows, d), BF16),
                            pltpu.SemaphoreType.DMA(())]),
        out_shape=out,
        compiler_params=_cparams(("arbitrary",)),
        name="moe_combine",
    )(sub_e.reshape(-1), sub_dst.reshape(-1), x, pos, wsel, ys, s1, s3, s2,
      ln_g.reshape(1, d), ln_b.reshape(1, d))


def _dispatch_plan(seg, buf_rows, n_sorted_rows):
    nt, ne = seg.shape
    n_sub = buf_rows // SUB_ROWS
    per_blk = FFN_ROWS // SUB_ROWS
    seg = seg.astype(jnp.int32)
    seg_end = jnp.cumsum(seg, axis=1)
    seg_start = seg_end - seg
    exp_sub = jnp.sum(seg, axis=0)
    exp_blk = (exp_sub + per_blk - 1) // per_blk
    blk_end = jnp.cumsum(exp_blk)
    exp_start = (blk_end - exp_blk) * per_blk
    dst_start = exp_start[None, :] + jnp.cumsum(seg, axis=0) - seg
    j = jnp.arange(n_sub, dtype=jnp.int32)
    sub_e = jnp.sum((seg_end[:, None, :] <= j[None, :, None]).astype(jnp.int32), axis=-1)
    used = j[None, :] < seg_end[:, -1:]
    sub_ec = jnp.minimum(sub_e, ne - 1)
    pick = sub_ec[:, :, None] == jnp.arange(ne, dtype=jnp.int32)[None, None, :]
    dst = j[None, :] + jnp.sum(jnp.where(pick, (dst_start - seg_start)[:, None, :], 0), axis=-1)
    sub_dst = jnp.where(used, dst, n_sorted_rows // SUB_ROWS + j[None, :]).astype(jnp.int32)
    sub_e = jnp.where(used, sub_ec, -1).astype(jnp.int32)
    nblk = (n_sorted_rows + buf_rows) // FFN_ROWS
    i = jnp.arange(nblk, dtype=jnp.int32)
    blk_e = jnp.minimum(jnp.sum((blk_end[None, :] <= i[:, None]).astype(jnp.int32), axis=-1), ne - 1)
    return sub_e, sub_dst, blk_e, blk_end[-1:].astype(jnp.int32)


def kernel(x, w_in, b_gate, hgrn_lb_logits, hgrn_norm_w, s5_a_re, s5_a_im, s5_b_re, s5_b_im,
           s5_c_re, s5_c_im, s5_d, s5_log_dt, s5_glu_w, s5_glu_b, rg_conv_w, rg_conv_b,
           rg_wa, rg_ba, rg_wx, rg_bx, rg_lambda, w_branch, w_out, ln1_g, ln1_b,
           router_w, router_bias, exp_w1, exp_w3, exp_w2, sh_w1, sh_w3, sh_w2, ln2_g, ln2_b):
    bn, s, d = x.shape
    depth = w_in.shape[0]
    n = bn * s
    w = hgrn_norm_w.shape[1]
    alpha = (2 * depth) ** 0.25

    sm = jax.nn.softmax(hgrn_lb_logits.astype(F32), axis=0)
    lower_bounds = jnp.cumsum(sm, axis=0) - sm[0:1]

    xt = x
    sorted_init = None
    seq_rows = 256 * bn
    c_hg = (0, 1, 2, 3)
    c_su = 4
    c_rg = (5, 6)
    c_gate = 7
    n_exp = router_w.shape[2]
    buf_rows = _tile_buffer_rows(MOE_TILE, n_exp, TOP_K)
    per_blk = FFN_ROWS // SUB_ROWS
    max_sub = n * TOP_K // SUB_ROWS + (n // MOE_TILE) * n_exp + n_exp * (per_blk - 1)
    n_sorted = -(-max_sub // per_blk) * FFN_ROWS
    for l in range(depth):
        proj = _in_proj(xt, w_in, l, bn, 2048, w)
        ya = _hgrn2(proj, c_hg, lower_bounds[l], hgrn_norm_w[l], bn, seq_rows)
        mats = _s5_matrices(s5_a_re[l], s5_a_im[l], s5_b_re[l], s5_b_im[l],
                            s5_c_re[l], s5_c_im[l], s5_log_dt[l])
        yb = _s5(proj, c_su, mats, s5_d[l], s5_glu_w[l], s5_glu_b[l], bn, seq_rows // 2)
        yc = _rglru(proj, c_rg, rg_conv_w[l], rg_conv_b[l], rg_wa[l], rg_ba[l], rg_wx[l], rg_bx[l],
                    jax.nn.softplus(-rg_lambda[l].astype(F32)), bn, seq_rows)
        x1 = _merge(xt, ya, yb, yc, proj, c_gate, w_branch, b_gate[l], w_out,
                    ln1_g[l], ln1_b[l], l, alpha, bn, 512)
        pos, wsel, seg = _router(x1, router_w[l].T, router_bias[l], bn, MOE_TILE)
        sub_e, sub_dst, blk_e, n_used = _dispatch_plan(seg[:, :, 0], buf_rows, n_sorted)
        if sorted_init is None:
            sorted_init = jnp.zeros((n_sorted + buf_rows, d), BF16)
        xs = _dispatch(x1, pos, sorted_init, sub_e, sub_dst, bn, buf_rows)
        ys = _experts(xs, blk_e, n_used, exp_w1, exp_w3, exp_w2, l)
        xt = _combine(x1, pos, wsel, ys, sub_e, sub_dst, sh_w1, sh_w3, sh_w2, ln2_g[l], ln2_b[l],
                      l, alpha, bn, buf_rows, batch_major_out=(l == depth - 1))
        sorted_init = ys
    return xt
```

```python
import functools
import math

import numpy as np
import jax
import jax.numpy as jnp
from jax import lax
from jax.experimental import pallas as pl
from jax.experimental.pallas import tpu as pltpu

F32 = jnp.float32
BF16 = jnp.bfloat16

HG_HEADS = 4
HG_HEAD_DIM = 128
HG_CHUNK = 16
S5_GROUP_CH = 16
S5_STATE = 64
RG_BLOCK_W = 128
RG_CONV = 4
RG_C = 8.0
N_EXPERTS = 64
N_EXPERT_GROUPS = 8
TOPK_GROUPS = 4
TOP_K = 8
ROUTED_SCALE = 2.5
LN_EPS = 1e-5
RMS_EPS = 1e-6
LANES = 128
SUBLANES = 8
VMEM_LIMIT = 56 * 1024 * 1024
SUB_ROWS = 16
MOE_TILE = 512
FFN_ROWS = 512
MM_ROWS = 512


def _cparams(sem):
    return pltpu.CompilerParams(dimension_semantics=sem, vmem_limit_bytes=VMEM_LIMIT)


def _sigmoid(x):
    return 1.0 / (1.0 + jnp.exp(-x))


def _silu(x):
    return x * _sigmoid(x)


def _gelu_tanh(x):
    c = math.sqrt(2.0 / math.pi)
    return 0.5 * x * (1.0 + jnp.tanh(c * (x + 0.044715 * (x * x * x))))


def _layernorm(z, g, b):
    mu = jnp.mean(z, axis=-1, keepdims=True)
    zc = z - mu
    var = jnp.mean(zc * zc, axis=-1, keepdims=True)
    return zc * lax.rsqrt(var + LN_EPS) * g + b


def _dot(a, b):
    return jnp.dot(a, b, preferred_element_type=F32)


def _interleave_batches(x_ref, slab_ref):
    bn, t, c = x_ref.shape
    for b in range(bn):
        for sl in range(c // LANES):
            slab_ref[sl, pl.ds(b, t, stride=bn), :] = x_ref[b, :, sl * LANES:(sl + 1) * LANES]
    return jnp.concatenate([slab_ref[sl] for sl in range(c // LANES)], axis=1)


def _split_batches(y, o_ref, slab_ref):
    bn, t, c = o_ref.shape
    for sl in range(c // LANES):
        slab_ref[sl] = y[:, sl * LANES:(sl + 1) * LANES]
    for b in range(bn):
        for sl in range(c // LANES):
            o_ref[b, :, sl * LANES:(sl + 1) * LANES] = slab_ref[sl, pl.ds(b, t, stride=bn), :]


def _matmul_kernel(x_ref, w_ref, o_ref, xb_ref):
    @pl.when(pl.program_id(1) == 0)
    def _():
        xb_ref[...] = x_ref[...].astype(BF16)

    o_ref[...] = _dot(xb_ref[...], w_ref[0].astype(BF16)).astype(o_ref.dtype)


def _matmul_bsd_kernel(x_ref, w_ref, o_ref, xt_ref, xb_ref, slab_ref):
    @pl.when(pl.program_id(1) == 0)
    def _():
        xt = _interleave_batches(x_ref, slab_ref)
        xt_ref[...] = xt
        xb_ref[...] = xt.astype(BF16)

    o_ref[...] = _dot(xb_ref[...], w_ref[0].astype(BF16)).astype(o_ref.dtype)


def _in_proj(x, w, layer, tm, tn):
    k, n = w.shape[1], w.shape[2]
    w_spec = pl.BlockSpec((1, k, tn), lambda i, j: (layer, 0, j))
    sem = _cparams(("parallel", "arbitrary"))
    if x.ndim == 2:
        m = x.shape[0]
        return pl.pallas_call(
            _matmul_kernel,
            grid=(m // tm, n // tn),
            in_specs=[pl.BlockSpec((tm, k), lambda i, j: (i, 0)), w_spec],
            out_specs=pl.BlockSpec((tm, tn), lambda i, j: (i, j)),
            out_shape=jax.ShapeDtypeStruct((m, n), BF16),
            scratch_shapes=[pltpu.VMEM((tm, k), BF16)],
            compiler_params=sem,
            name="in_proj",
        )(x, w), x
    bn, s, _ = x.shape
    m = bn * s
    return pl.pallas_call(
        _matmul_bsd_kernel,
        grid=(m // tm, n // tn),
        in_specs=[pl.BlockSpec((bn, tm // bn, k), lambda i, j: (0, i, 0)), w_spec],
        out_specs=[pl.BlockSpec((tm, tn), lambda i, j: (i, j)),
                   pl.BlockSpec((tm, k), lambda i, j: (i, 0))],
        out_shape=[jax.ShapeDtypeStruct((m, n), BF16), jax.ShapeDtypeStruct((m, k), F32)],
        scratch_shapes=[pltpu.VMEM((tm, k), BF16), pltpu.VMEM((k // LANES, tm, LANES), F32)],
        compiler_params=sem,
        name="in_proj_bsd",
    )(x, w)


def _hgrn2_kernel(q_ref, f_ref, v_ref, g_ref, lb_ref, nw_ref, o_ref, st_ref, kvb_ref, *, batch):
    rows = q_ref.shape[0]
    cr = HG_CHUNK * batch
    n_chunks = rows // cr
    dh = HG_HEAD_DIM

    @pl.when(pl.program_id(0) == 0)
    def _():
        st_ref[...] = jnp.zeros_like(st_ref)
        kvb_ref[...] = jnp.zeros_like(kvb_ref)

    row = lax.broadcasted_iota(jnp.int32, (cr, dh), 0)
    row_b = row % batch
    ones_sum = jnp.ones((dh, dh), BF16)

    def chunk(c, carry):
        r0 = pl.multiple_of(c * cr, cr)
        for h in range(HG_HEADS):
            ls = slice(h * dh, (h + 1) * dh)
            lb = lb_ref[:, ls]
            f = lb + (1.0 - lb) * _sigmoid(f_ref[pl.ds(r0, cr), ls].astype(F32))
            q = _silu(q_ref[pl.ds(r0, cr), ls].astype(F32))
            k = 1.0 - f
            v = v_ref[pl.ds(r0, cr), ls].astype(F32)
            bc = jnp.log(f)
            sh = batch
            while sh < cr:
                bc = bc + jnp.where(row >= sh, pltpu.roll(bc, sh, 0), 0.0)
                sh *= 2
            p0 = (q * k).astype(BF16)
            o = _dot(p0, ones_sum) * v
            for idx, a in enumerate((k, v, bc)):
                kvb_ref[h, idx, SUBLANES:SUBLANES + cr, :] = a
            for j in range(1, HG_CHUNK):
                s = j * batch
                lo = SUBLANES * (s // SUBLANES)
                n = cr - lo
                start = SUBLANES - (s - lo)
                ks, vs, bs = (kvb_ref[h, idx, start:start + n, :] for idx in range(3))
                p = q[lo:cr] * ks * jnp.exp(bc[lo:cr] - bs)
                if s != lo:
                    p = jnp.where(row[0:n] >= batch, p, 0.0)
                upd = _dot(p.astype(BF16), ones_sum) * vs
                o = jnp.concatenate([o[0:lo], o[lo:cr] + upd], axis=0) if lo else o + upd
            b_last = bc[cr - batch:cr, :]
            qt = q * jnp.exp(bc)
            kt = k * jnp.exp(jnp.concatenate([b_last] * HG_CHUNK, axis=0) - bc)
            qm = jnp.concatenate([jnp.where(row_b == b, qt, 0.0) for b in range(batch)], axis=1)
            km = jnp.concatenate([jnp.where(row_b == b, kt, 0.0) for b in range(batch)], axis=1)
            st = st_ref[h]
            o = o + lax.dot_general(qm.astype(BF16), st.astype(BF16),
                                    (((1,), (1,)), ((), ())), preferred_element_type=F32)
            kv = lax.dot_general(v.astype(BF16), km.astype(BF16),
                                 (((0,), (0,)), ((), ())), preferred_element_type=F32)
            dec_all = jnp.concatenate([jnp.exp(b_last[b:b + 1, :]) for b in range(batch)], axis=1)
            st_ref[h] = st * dec_all + kv
            o = o * lax.rsqrt(jnp.mean(o * o, axis=-1, keepdims=True) + RMS_EPS)
            o_ref[pl.ds(r0, cr), ls] = o * nw_ref[:, ls] * _silu(g_ref[pl.ds(r0, cr), ls].astype(F32))
        return carry

    lax.fori_loop(0, n_chunks, chunk, 0)


def _hgrn2(proj, cols, lb, norm_w, batch, rows):
    assert 2 * batch == SUBLANES, "two time steps must fill the 8 sublanes"
    n = proj.shape[0]
    w = HG_HEADS * HG_HEAD_DIM
    cq, cf, cv, cg = cols

    def spec(cb):
        return pl.BlockSpec((rows, w), lambda i: (i, cb))

    vec = pl.BlockSpec((1, w), lambda i: (0, 0))
    return pl.pallas_call(
        functools.partial(_hgrn2_kernel, batch=batch),
        grid=(n // rows,),
        in_specs=[spec(cq), spec(cf), spec(cv), spec(cg), vec, vec],
        out_specs=pl.BlockSpec((rows, w), lambda i: (i, 0)),
        out_shape=jax.ShapeDtypeStruct((n, w), F32),
        scratch_shapes=[pltpu.VMEM((HG_HEADS, HG_HEAD_DIM, batch * HG_HEAD_DIM), F32),
                        pltpu.VMEM((HG_HEADS, 3, SUBLANES + HG_CHUNK * batch, HG_HEAD_DIM), F32)],
        compiler_params=_cparams(("arbitrary",)),
        name="hgrn2",
    )(proj, proj, proj, proj, lb.reshape(1, w), norm_w.reshape(1, w))


S5_TILE = 128


def _s5_kernel(u_ref, pin_ref, bm_ref, ar_ref, ai_ref, qout_ref, cm_ref, d_ref, gw_ref, gb_ref,
               o_ref, hs_ref, h_ref, *, batch):
    rows, w = u_ref.shape
    nblk = w // LANES
    nc = hs_ref.shape[1]
    cpb = nc // nblk
    tile2 = 2 * S5_TILE
    n_tiles = rows // S5_TILE
    steps = rows // batch

    @pl.when(pl.program_id(0) == 0)
    def _():
        h_ref[...] = jnp.zeros_like(h_ref)

    for t in range(n_tiles):
        ub = u_ref[t * S5_TILE:(t + 1) * S5_TILE, :].astype(BF16)
        up = _dot(pin_ref[...], ub).astype(BF16)
        for j in range(nblk):
            ls = slice(j * LANES, (j + 1) * LANES)
            lhs = jnp.concatenate([up[0:tile2, ls], up[tile2:2 * tile2, ls]], axis=1)
            hs_ref[t * tile2:(t + 1) * tile2, j * cpb:(j + 1) * cpb] = _dot(lhs, bm_ref[j])

    top = lax.broadcasted_iota(jnp.int32, (2 * batch, LANES), 0) < batch
    groups = nc // LANES
    per_pass = 8
    unroll = 4
    for p in range(groups // per_pass):
        cols = [(p * per_pass + g) * LANES for g in range(per_pass)]
        ar8 = [jnp.broadcast_to(ar_ref[:, c:c + LANES], (2 * batch, LANES)) for c in cols]
        ai8 = [jnp.where(top, -1.0, 1.0) * ai_ref[:, c:c + LANES] for c in cols]

        def body(i, hs):
            hs = list(hs)
            for s in range(unroll):
                r0 = pl.multiple_of((i * unroll + s) * (2 * batch), 2 * batch)
                for g, c in enumerate(cols):
                    h = (ar8[g] * hs[g] + ai8[g] * pltpu.roll(hs[g], batch, 0)
                         + hs_ref[pl.ds(r0, 2 * batch), c:c + LANES])
                    hs_ref[pl.ds(r0, 2 * batch), c:c + LANES] = h
                    hs[g] = h
            return tuple(hs)

        h0 = tuple(h_ref[:, c:c + LANES] for c in cols)
        hn = lax.fori_loop(0, steps // unroll, body, h0)
        for g, c in enumerate(cols):
            h_ref[:, c:c + LANES] = hn[g]

    for t in range(n_tiles):
        hb = hs_ref[t * tile2:(t + 1) * tile2, :].astype(BF16)
        hre = _dot(qout_ref[0], hb).astype(BF16)
        him = _dot(qout_ref[1], hb).astype(BF16)
        ys = []
        for j in range(nblk):
            cs = slice(j * cpb, (j + 1) * cpb)
            ys.append(_dot(jnp.concatenate([hre[:, cs], him[:, cs]], axis=1), cm_ref[j]))
        rs = slice(t * S5_TILE, (t + 1) * S5_TILE)
        y = jnp.concatenate(ys, axis=1) + d_ref[...] * u_ref[rs, :].astype(F32)
        y = _gelu_tanh(y)
        z = _dot(y.astype(BF16), gw_ref[...].astype(BF16)) + gb_ref[...]
        o_ref[rs, :] = y * _sigmoid(z)


def _s5_layout_matrices(batch):
    assert 2 * batch == SUBLANES, "one time step must fill the 8 sublanes"
    tile2 = 2 * S5_TILE
    r2 = np.arange(tile2)
    src = (r2 // (2 * batch)) * batch + r2 % batch
    is_re = (r2 % (2 * batch)) < batch
    onehot = (src[:, None] == np.arange(S5_TILE)[None, :])
    p_re = onehot & is_re[:, None]
    p_im = onehot & ~is_re[:, None]
    pin = np.concatenate([p_re, p_im], axis=0).astype(np.float32)
    qout = np.stack([p_re.T, p_im.T], axis=0).astype(np.float32)
    return jnp.asarray(pin, BF16), jnp.asarray(qout, BF16)


def _s5(proj, col, mats, d, glu_w, glu_b, batch, rows):
    n = proj.shape[0]
    w = d.shape[0]
    bmat, ar, ai, cmat = mats
    nblk, _, cpb = bmat.shape
    nc = nblk * cpb
    pin, qout = _s5_layout_matrices(batch)
    full = lambda a: pl.BlockSpec(a.shape, lambda i, nd=a.ndim: (0,) * nd)
    return pl.pallas_call(
        functools.partial(_s5_kernel, batch=batch),
        grid=(n // rows,),
        in_specs=[pl.BlockSpec((rows, w), lambda i: (i, col)),
                  full(pin), full(bmat), full(ar), full(ai), full(qout), full(cmat),
                  pl.BlockSpec((1, w), lambda i: (0, 0)),
                  pl.BlockSpec((w, w), lambda i: (0, 0)),
                  pl.BlockSpec((1, w), lambda i: (0, 0))],
        out_specs=pl.BlockSpec((rows, w), lambda i: (i, 0)),
        out_shape=jax.ShapeDtypeStruct((n, w), F32),
        scratch_shapes=[pltpu.VMEM((2 * rows, nc), F32),
                        pltpu.VMEM((2 * batch, nc), F32)],
        compiler_params=_cparams(("arbitrary",)),
        name="s5",
    )(proj, pin, bmat, ar, ai, qout, cmat, d.reshape(1, w), glu_w, glu_b.reshape(1, w))


def _s5_matrices(a_re, a_im, b_re, b_im, c_re, c_im, log_dt):
    g, p = a_re.shape
    ch = b_re.shape[-1]
    gpb = LANES // ch
    nblk = g // gpb
    dt = jnp.exp(log_dt.astype(F32))[:, None]
    mag = jnp.exp(a_re * dt)
    abr = mag * jnp.cos(a_im * dt)
    abi = mag * jnp.sin(a_im * dt)
    den = a_re * a_re + a_im * a_im
    cr = ((abr - 1.0) * a_re + abi * a_im) / den
    ci = (abi * a_re - (abr - 1.0) * a_im) / den
    bbr = cr[..., None] * b_re - ci[..., None] * b_im
    bbi = cr[..., None] * b_im + ci[..., None] * b_re
    eye = jnp.eye(gpb, dtype=F32)

    def in_block(m):
        m = m.reshape(nblk, gpb, p, ch)
        return jnp.einsum('jgpc,gh->jgchp', m, eye).reshape(nblk, gpb * ch, gpb * p)

    def out_block(m):
        m = m.reshape(nblk, gpb, ch, p)
        return jnp.einsum('jgcp,gh->jgphc', m, eye).reshape(nblk, gpb * p, gpb * ch)

    bmat = jnp.concatenate([in_block(bbr), in_block(bbi)], axis=1).astype(BF16)
    cmat = jnp.concatenate([out_block(c_re), -out_block(c_im)], axis=1).astype(BF16)
    return bmat, abr.reshape(1, g * p), abi.reshape(1, g * p), cmat


def _rglru_kernel(xg_ref, xr_ref, cw_ref, cb_ref, wa_ref, ba_ref, wx_ref, bx_ref, sp_ref,
                  o_ref, xp_ref, a_ref, b_ref, h_ref, *, batch):
    rows, w = xr_ref.shape
    halo = SUBLANES * ((RG_CONV - 1) * batch // SUBLANES + 1)
    pair = 2 * batch

    @pl.when(pl.program_id(0) == 0)
    def _():
        xp_ref[0:halo, :] = jnp.zeros((halo, w), F32)
        h_ref[...] = jnp.zeros_like(h_ref)

    xr = xr_ref[...].astype(F32)
    xp_ref[halo:halo + rows, :] = xr
    xc = cb_ref[...] + cw_ref[RG_CONV - 1:RG_CONV, :] * xr
    for i in range(RG_CONV - 1):
        back = (RG_CONV - 1 - i) * batch
        xc = xc + cw_ref[i:i + 1, :] * xp_ref[halo - back:halo - back + rows, :]
    xp_ref[0:halo, :] = xr[rows - halo:rows, :]

    xcb = xc.astype(BF16)
    nb = w // RG_BLOCK_W
    ra, ri = [], []
    for hblk in range(nb):
        xs = xcb[:, hblk * RG_BLOCK_W:(hblk + 1) * RG_BLOCK_W]
        ra.append(_dot(xs, wa_ref[hblk].astype(BF16)))
        ri.append(_dot(xs, wx_ref[hblk].astype(BF16)))
    r = _sigmoid(jnp.concatenate(ra, axis=1) + ba_ref[...])
    ig = _sigmoid(jnp.concatenate(ri, axis=1) + bx_ref[...])
    log_a = (-RG_C) * r * sp_ref[...]
    a = jnp.exp(log_a)
    a_ref[...] = a
    b_ref[...] = jnp.sqrt(-jnp.tanh(log_a) * (a * a + 1.0)) * (ig * xc)

    top = lax.broadcasted_iota(jnp.int32, (pair, w), 0) < batch
    unroll = 4

    def body(i, h):
        for s in range(unroll):
            r0 = pl.multiple_of((i * unroll + s) * pair, pair)
            at = a_ref[pl.ds(r0, pair), :]
            bt = b_ref[pl.ds(r0, pair), :]
            n1 = at * h + bt
            h1 = jnp.where(top, n1, pltpu.roll(n1, batch, 0))
            n2 = at * h1 + bt
            b_ref[pl.ds(r0, pair), :] = jnp.where(top, n1, n2)
            h = jnp.where(top, pltpu.roll(n2, batch, 0), n2)
        return h

    h_ref[...] = lax.fori_loop(0, rows // (pair * unroll), body, h_ref[...])
    o_ref[...] = b_ref[...] * _gelu_tanh(xg_ref[...].astype(F32))


def _rglru(proj, cols, conv_w, conv_b, wa, ba, wx, bx, softplus_neg_lam, batch, rows):
    assert 2 * batch == SUBLANES, "two time steps must fill the 8 sublanes"
    n = proj.shape[0]
    w = conv_b.shape[0]
    cg, cx = cols
    nb = wa.shape[0]
    halo = SUBLANES * ((RG_CONV - 1) * batch // SUBLANES + 1)
    vec = pl.BlockSpec((1, w), lambda i: (0, 0))
    blk = pl.BlockSpec((nb, RG_BLOCK_W, RG_BLOCK_W), lambda i: (0, 0, 0))
    return pl.pallas_call(
        functools.partial(_rglru_kernel, batch=batch),
        grid=(n // rows,),
        in_specs=[pl.BlockSpec((rows, w), lambda i: (i, cg)),
                  pl.BlockSpec((rows, w), lambda i: (i, cx)),
                  pl.BlockSpec((RG_CONV, w), lambda i: (0, 0)),
                  vec, blk, vec, blk, vec, vec],
        out_specs=pl.BlockSpec((rows, w), lambda i: (i, 0)),
        out_shape=jax.ShapeDtypeStruct((n, w), F32),
        scratch_shapes=[pltpu.VMEM((halo + rows, w), F32),
                        pltpu.VMEM((rows, w), F32),
                        pltpu.VMEM((rows, w), F32),
                        pltpu.VMEM((2 * batch, w), F32)],
        compiler_params=_cparams(("arbitrary",)),
        name="rglru",
    )(proj, proj, conv_w, conv_b.reshape(1, w), wa, ba.reshape(1, w), wx, bx.reshape(1, w),
      softplus_neg_lam.reshape(1, w))


def _merge_kernel(x_ref, ya_ref, yb_ref, yc_ref, *rest, alpha, halves):
    n_gate = 3 * halves
    gate_refs = rest[:n_gate]
    wb_ref, bg_ref, wo_ref, lg_ref, lb_ref, o_ref = rest[n_gate:]
    m = None
    for kbr, y_ref in enumerate((ya_ref, yb_ref, yc_ref)):
        br = _dot(y_ref[...].astype(BF16), wb_ref[0, kbr].astype(BF16))
        gp = jnp.concatenate([gate_refs[kbr * halves + i][...] for i in range(halves)], axis=1)
        t = _sigmoid(gp.astype(F32) + bg_ref[kbr:kbr + 1, :]) * br
        m = t if m is None else m + t
    mix = _dot(m.astype(BF16), wo_ref[0].astype(BF16))
    o_ref[...] = _layernorm(alpha * x_ref[...] + mix, lg_ref[...], lb_ref[...])


def _merge(x, ya, yb, yc, proj, gate_col0, w_branch, b_gate, w_out, ln_g, ln_b, layer, alpha, tm):
    n, d = x.shape
    w = ya.shape[1]
    nbr = w_branch.shape[1]
    halves = d // w
    row = lambda width: pl.BlockSpec((tm, width), lambda i: (i, 0))
    gates = [pl.BlockSpec((tm, w), lambda i, c=gate_col0 + c: (i, c)) for c in range(nbr * halves)]
    vec = pl.BlockSpec((1, d), lambda i: (0, 0))
    return pl.pallas_call(
        functools.partial(_merge_kernel, alpha=alpha, halves=halves),
        grid=(n // tm,),
        in_specs=[row(d), row(w), row(w), row(w)] + gates + [
            pl.BlockSpec((1, nbr, w, d), lambda i: (layer, 0, 0, 0)),
            pl.BlockSpec((nbr, d), lambda i: (0, 0)),
            pl.BlockSpec((1, d, d), lambda i: (layer, 0, 0)), vec, vec],
        out_specs=row(d),
        out_shape=jax.ShapeDtypeStruct((n, d), F32),
        compiler_params=_cparams(("parallel",)),
        name="merge",
    )(x, ya, yb, yc, *([proj] * (nbr * halves)), w_branch, b_gate.reshape(nbr, d), w_out,
      ln_g.reshape(1, d), ln_b.reshape(1, d))


def _first_max_mask(cur, idx, axis):
    m = jnp.max(cur, axis=axis, keepdims=True)
    first = jnp.min(jnp.where(cur == m, idx, jnp.int32(2 ** 30)), axis=axis, keepdims=True)
    return idx == first


def _router_kernel(x_ref, wr_ref, rb_ref, pos_ref, w_ref, seg_ref):
    tm = x_ref.shape[0]
    e = N_EXPERTS
    per = e // N_EXPERT_GROUPS
    x = x_ref[...]
    wr = wr_ref[...]
    xh = x.astype(BF16)
    xl = (x - xh.astype(F32)).astype(BF16)
    wh = wr.astype(BF16)
    wl = (wr - wh.astype(F32)).astype(BF16)
    nt = (((1,), (1,)), ((), ()))
    logits = (lax.dot_general(wh, xh, nt, preferred_element_type=F32)
              + lax.dot_general(wh, xl, nt, preferred_element_type=F32)
              + lax.dot_general(wl, xh, nt, preferred_element_type=F32))
    scores = _sigmoid(logits)
    biased = scores + rb_ref[...]
    neg = jnp.float32(-jnp.inf)

    b3 = biased.reshape(N_EXPERT_GROUPS, per, tm)
    i3 = lax.broadcasted_iota(jnp.int32, b3.shape, 1)
    top1 = _first_max_mask(b3, i3, 1)
    m1 = jnp.max(b3, axis=1, keepdims=True)
    m2 = jnp.max(jnp.where(top1, neg, b3), axis=1, keepdims=True)
    gscore = (m1 + m2).reshape(N_EXPERT_GROUPS, tm)

    ig = lax.broadcasted_iota(jnp.int32, gscore.shape, 0)
    gsel = jnp.zeros(gscore.shape, F32)
    cur = gscore
    for _ in range(TOPK_GROUPS):
        pick = _first_max_mask(cur, ig, 0)
        gsel = jnp.where(pick, 1.0, gsel)
        cur = jnp.where(pick, neg, cur)

    gsel3 = jnp.broadcast_to(gsel.reshape(N_EXPERT_GROUPS, 1, tm), b3.shape)
    masked = jnp.where(gsel3 > 0.0, b3, neg).reshape(e, tm)
    ie = lax.broadcasted_iota(jnp.int32, masked.shape, 0)
    chosen = jnp.zeros(masked.shape, F32)
    cur = masked
    for _ in range(TOP_K):
        pick = _first_max_mask(cur, ie, 0)
        chosen = jnp.where(pick, 1.0, chosen)
        cur = jnp.where(pick, neg, cur)

    wsel = jnp.where(chosen > 0.0, scores, 0.0)
    wsel = wsel / jnp.sum(wsel, axis=0, keepdims=True) * ROUTED_SCALE
    w_ref[0] = wsel

    cb = chosen.astype(BF16)
    tok_r = lax.broadcasted_iota(jnp.int32, (tm, tm), 0)
    tok_c = lax.broadcasted_iota(jnp.int32, (tm, tm), 1)
    before = jnp.where(tok_r < tok_c, 1.0, 0.0).astype(BF16)
    prefix = _dot(cb, before)
    cnt = jnp.sum(chosen, axis=1, keepdims=True)
    seg = jnp.floor((cnt + (SUB_ROWS - 1)) * (1.0 / SUB_ROWS))
    seg_b = jnp.broadcast_to(seg, (e, LANES))
    ex_r = lax.broadcasted_iota(jnp.int32, (e, e), 0)
    ex_c = lax.broadcasted_iota(jnp.int32, (e, e), 1)
    earlier = jnp.where(ex_c < ex_r, 1.0, 0.0).astype(BF16)
    seg_off = _dot(earlier, seg_b.astype(BF16))
    seg_ref[0] = seg_b
    pos_ref[0] = jnp.where(chosen > 0.0, seg_off[:, 0:1] * SUB_ROWS + prefix, -1.0)


def _router(x, router_w_t, router_bias, tm):
    n, d = x.shape
    e = router_w_t.shape[0]
    nt = n // tm
    tile = lambda width: pl.BlockSpec((1, e, width), lambda i: (i, 0, 0))
    return pl.pallas_call(
        _router_kernel,
        grid=(nt,),
        in_specs=[pl.BlockSpec((tm, d), lambda i: (i, 0)),
                  pl.BlockSpec((e, d), lambda i: (0, 0)),
                  pl.BlockSpec((e, 1), lambda i: (0, 0))],
        out_specs=[tile(tm), tile(tm), tile(LANES)],
        out_shape=[jax.ShapeDtypeStruct((nt, e, tm), F32),
                   jax.ShapeDtypeStruct((nt, e, tm), F32),
                   jax.ShapeDtypeStruct((nt, e, LANES), F32)],
        compiler_params=_cparams(("parallel",)),
        name="router",
    )(x, router_w_t, router_bias.reshape(e, 1))


def _tile_buffer_rows(tile, n_experts, top_k):
    rows = tile * top_k + n_experts * (SUB_ROWS - 1)
    return -(-rows // MM_ROWS) * MM_ROWS


def _slot_matrix(j, e, pos_ref, val_ref, tile):
    e = jnp.maximum(e, 0)
    prow = pos_ref[0, pl.ds(e, 1), :]
    tgt = (lax.broadcasted_iota(jnp.int32, (SUB_ROWS, tile), 0) + j * SUB_ROWS).astype(F32)
    if val_ref is None:
        return jnp.where(prow == tgt, 1.0, 0.0).astype(BF16)
    return jnp.where(prow == tgt, val_ref[0, pl.ds(e, 1), :], 0.0).astype(BF16)


LOOP_UNROLL = 8


def _wait_sub_blocks(src, dst, sem, n_sub):
    def wait(i, c):
        for _ in range(LOOP_UNROLL):
            pltpu.make_async_copy(src.at[pl.ds(0, SUB_ROWS)], dst.at[pl.ds(0, SUB_ROWS)], sem).wait()
        return c

    lax.fori_loop(0, n_sub // LOOP_UNROLL, wait, 0)


def _dispatch_kernel(sub_e_ref, sub_dst_ref, x_ref, pos_ref, init_ref, xs_hbm, m_ref, xs_ref, sem):
    del init_ref
    t = pl.program_id(0)
    tile = x_ref.shape[0]
    n_sub = m_ref.shape[0] // SUB_ROWS
    base = t * n_sub

    def build(j, c):
        r0 = pl.multiple_of(j * SUB_ROWS, SUB_ROWS)
        m_ref[pl.ds(r0, SUB_ROWS), :] = _slot_matrix(j, sub_e_ref[base + j], pos_ref, None, tile)
        return c

    lax.fori_loop(0, n_sub, build, 0, unroll=LOOP_UNROLL)
    xb = x_ref[...].astype(BF16)
    for c in range(m_ref.shape[0] // MM_ROWS):
        rs = slice(c * MM_ROWS, (c + 1) * MM_ROWS)
        xs_ref[rs, :] = _dot(m_ref[rs, :], xb).astype(BF16)

    def start(j, c):
        r0 = pl.multiple_of(j * SUB_ROWS, SUB_ROWS)
        d0 = pl.multiple_of(sub_dst_ref[base + j] * SUB_ROWS, SUB_ROWS)
        pltpu.make_async_copy(xs_ref.at[pl.ds(r0, SUB_ROWS)], xs_hbm.at[pl.ds(d0, SUB_ROWS)], sem).start()
        return c

    lax.fori_loop(0, n_sub, start, 0, unroll=LOOP_UNROLL)
    _wait_sub_blocks(xs_ref, xs_hbm, sem, n_sub)


def _dispatch(x, pos, init, sub_e, sub_dst, buf_rows):
    nt, e, tile = pos.shape
    d = init.shape[1]
    return pl.pallas_call(
        _dispatch_kernel,
        grid_spec=pltpu.PrefetchScalarGridSpec(
            num_scalar_prefetch=2,
            grid=(nt,),
            in_specs=[pl.BlockSpec((tile, d), lambda i, *_: (i, 0)),
                      pl.BlockSpec((1, e, tile), lambda i, *_: (i, 0, 0)),
                      pl.BlockSpec(memory_space=pl.ANY)],
            out_specs=pl.BlockSpec(memory_space=pl.ANY),
            scratch_shapes=[pltpu.VMEM((buf_rows, tile), BF16),
                            pltpu.VMEM((buf_rows, d), BF16),
                            pltpu.SemaphoreType.DMA(())]),
        out_shape=jax.ShapeDtypeStruct(init.shape, BF16),
        input_output_aliases={4: 0},
        compiler_params=_cparams(("arbitrary",)),
        name="moe_dispatch",
    )(sub_e.reshape(-1), sub_dst.reshape(-1), x, pos, init)


def _experts_kernel(blk_e_ref, n_used_ref, xs_ref, w1_ref, w3_ref, w2_ref, ys_ref, w13_ref, w2b_ref):
    i = pl.program_id(0)
    hid = w1_ref.shape[-1]
    used = i < n_used_ref[0]
    new_expert = jnp.logical_or(i == 0, blk_e_ref[i] != blk_e_ref[jnp.maximum(i - 1, 0)])

    @pl.when(jnp.logical_and(used, new_expert))
    def _():
        w13_ref[:, 0:hid] = w1_ref[0, 0].astype(BF16)
        w13_ref[:, hid:2 * hid] = w3_ref[0, 0].astype(BF16)
        w2b_ref[...] = w2_ref[0, 0].astype(BF16)

    @pl.when(used)
    def _():
        h = _dot(xs_ref[...], w13_ref[...])
        hh = (_silu(h[:, 0:hid]) * h[:, hid:2 * hid]).astype(BF16)
        ys_ref[...] = _dot(hh, w2b_ref[...]).astype(ys_ref.dtype)

    @pl.when(jnp.logical_not(used))
    def _():
        ys_ref[...] = jnp.zeros_like(ys_ref)


def _experts(xs, blk_e, n_used, w1, w3, w2, layer):
    rows, d = xs.shape
    hid = w1.shape[-1]
    nblk = rows // FFN_ROWS
    return pl.pallas_call(
        _experts_kernel,
        grid_spec=pltpu.PrefetchScalarGridSpec(
            num_scalar_prefetch=2,
            grid=(nblk,),
            in_specs=[pl.BlockSpec((FFN_ROWS, d), lambda i, be, nu: (jnp.minimum(i, nu[0] - 1), 0)),
                      pl.BlockSpec((1, 1, d, hid), lambda i, be, nu: (layer, be[i], 0, 0)),
                      pl.BlockSpec((1, 1, d, hid), lambda i, be, nu: (layer, be[i], 0, 0)),
                      pl.BlockSpec((1, 1, hid, d), lambda i, be, nu: (layer, be[i], 0, 0))],
            out_specs=pl.BlockSpec((FFN_ROWS, d), lambda i, be, nu: (i, 0)),
            scratch_shapes=[pltpu.VMEM((d, 2 * hid), BF16), pltpu.VMEM((hid, d), BF16)]),
        out_shape=jax.ShapeDtypeStruct((rows, d), BF16),
        compiler_params=_cparams(("arbitrary",)),
        name="moe_experts",
    )(blk_e, n_used, xs, w1, w3, w2)


def _combine_kernel(sub_e_ref, sub_dst_ref, x_ref, pos_ref, w_ref, ys_hbm, s1_ref, s3_ref, s2_ref,
                    lg_ref, lb_ref, o_ref, m_ref, ys_ref, sem, *maybe_slab, alpha):
    t = pl.program_id(0)
    tile = x_ref.shape[0]
    n_sub = m_ref.shape[0] // SUB_ROWS
    base = t * n_sub

    def start(j, c):
        r0 = pl.multiple_of(j * SUB_ROWS, SUB_ROWS)
        d0 = pl.multiple_of(sub_dst_ref[base + j] * SUB_ROWS, SUB_ROWS)
        pltpu.make_async_copy(ys_hbm.at[pl.ds(d0, SUB_ROWS)], ys_ref.at[pl.ds(r0, SUB_ROWS)], sem).start()
        return c

    def build(j, c):
        r0 = pl.multiple_of(j * SUB_ROWS, SUB_ROWS)
        m_ref[pl.ds(r0, SUB_ROWS), :] = _slot_matrix(j, sub_e_ref[base + j], pos_ref, w_ref, tile)
        return c

    lax.fori_loop(0, n_sub, start, 0, unroll=LOOP_UNROLL)
    lax.fori_loop(0, n_sub, build, 0, unroll=LOOP_UNROLL)
    _wait_sub_blocks(ys_hbm, ys_ref, sem, n_sub)

    x = x_ref[...]
    xb = x.astype(BF16)
    g1 = _dot(xb, s1_ref[0].astype(BF16))
    g3 = _dot(xb, s3_ref[0].astype(BF16))
    ffn = _dot((_silu(g1) * g3).astype(BF16), s2_ref[0].astype(BF16))
    tn = (((0,), (0,)), ((), ()))
    for c in range(m_ref.shape[0] // MM_ROWS):
        rs = slice(c * MM_ROWS, (c + 1) * MM_ROWS)
        ffn = ffn + lax.dot_general(m_ref[rs, :], ys_ref[rs, :], tn, preferred_element_type=F32)
    y = _layernorm(alpha * x + ffn, lg_ref[...], lb_ref[...])
    if maybe_slab:
        _split_batches(y, o_ref, maybe_slab[0])
    else:
        o_ref[...] = y


def _combine(x, pos, wsel, ys, sub_e, sub_dst, s1, s3, s2, ln_g, ln_b, layer, alpha, buf_rows, out_batch):
    n, d = x.shape
    nt, e, tile = pos.shape
    hid = s1.shape[-1]
    vec = pl.BlockSpec((1, d), lambda i, *_: (0, 0))
    scratch = [pltpu.VMEM((buf_rows, tile), BF16), pltpu.VMEM((buf_rows, d), BF16),
               pltpu.SemaphoreType.DMA(())]
    if out_batch:
        out = jax.ShapeDtypeStruct((out_batch, n // out_batch, d), F32)
        out_spec = pl.BlockSpec((out_batch, tile // out_batch, d), lambda i, *_: (0, i, 0))
        scratch.append(pltpu.VMEM((d // LANES, tile, LANES), F32))
    else:
        out = jax.ShapeDtypeStruct((n, d), F32)
        out_spec = pl.BlockSpec((tile, d), lambda i, *_: (i, 0))
    return pl.pallas_call(
        functools.partial(_combine_kernel, alpha=alpha),
        grid_spec=pltpu.PrefetchScalarGridSpec(
            num_scalar_prefetch=2,
            grid=(nt,),
            in_specs=[pl.BlockSpec((tile, d), lambda i, *_: (i, 0)),
                      pl.BlockSpec((1, e, tile), lambda i, *_: (i, 0, 0)),
                      pl.BlockSpec((1, e, tile), lambda i, *_: (i, 0, 0)),
                      pl.BlockSpec(memory_space=pl.ANY),
                      pl.BlockSpec((1, d, hid), lambda i, *_: (layer, 0, 0)),
                      pl.BlockSpec((1, d, hid), lambda i, *_: (layer, 0, 0)),
                      pl.BlockSpec((1, hid, d), lambda i, *_: (layer, 0, 0)), vec, vec],
            out_specs=out_spec,
            scratch_shapes=scratch),
        out_shape=out,
        compiler_params=_cparams(("arbitrary",)),
        name="moe_combine",
    )(sub_e.reshape(-1), sub_dst.reshape(-1), x, pos, wsel, ys, s1, s3, s2,
      ln_g.reshape(1, d), ln_b.reshape(1, d))


def _dispatch_plan(seg, buf_rows, n_sorted_rows):
    nt, ne = seg.shape
    n_sub = buf_rows // SUB_ROWS
    per_blk = FFN_ROWS // SUB_ROWS
    seg = seg.astype(jnp.int32)
    seg_end = jnp.cumsum(seg, axis=1)
    seg_start = seg_end - seg
    exp_sub = jnp.sum(seg, axis=0)
    exp_blk = (exp_sub + per_blk - 1) // per_blk
    blk_end = jnp.cumsum(exp_blk)
    exp_start = (blk_end - exp_blk) * per_blk
    dst_start = exp_start[None, :] + jnp.cumsum(seg, axis=0) - seg
    j = jnp.arange(n_sub, dtype=jnp.int32)
    sub_e = jnp.sum((seg_end[:, None, :] <= j[None, :, None]).astype(jnp.int32), axis=-1)
    used = j[None, :] < seg_end[:, -1:]
    sub_ec = jnp.minimum(sub_e, ne - 1)
    pick = sub_ec[:, :, None] == jnp.arange(ne, dtype=jnp.int32)[None, None, :]
    dst = j[None, :] + jnp.sum(jnp.where(pick, (dst_start - seg_start)[:, None, :], 0), axis=-1)
    sub_dst = jnp.where(used, dst, n_sorted_rows // SUB_ROWS + j[None, :]).astype(jnp.int32)
    sub_e = jnp.where(used, sub_ec, -1).astype(jnp.int32)
    nblk = (n_sorted_rows + buf_rows) // FFN_ROWS
    i = jnp.arange(nblk, dtype=jnp.int32)
    blk_e = jnp.minimum(jnp.sum((blk_end[None, :] <= i[:, None]).astype(jnp.int32), axis=-1), ne - 1)
    return sub_e, sub_dst, blk_e, blk_end[-1:].astype(jnp.int32)


def kernel(x, w_in, b_gate, hgrn_lb_logits, hgrn_norm_w, s5_a_re, s5_a_im, s5_b_re, s5_b_im,
           s5_c_re, s5_c_im, s5_d, s5_log_dt, s5_glu_w, s5_glu_b, rg_conv_w, rg_conv_b,
           rg_wa, rg_ba, rg_wx, rg_bx, rg_lambda, w_branch, w_out, ln1_g, ln1_b,
           router_w, router_bias, exp_w1, exp_w3, exp_w2, sh_w1, sh_w3, sh_w2, ln2_g, ln2_b):
    bn, s, d = x.shape
    depth = w_in.shape[0]
    n = bn * s
    w = hgrn_norm_w.shape[1]
    alpha = (2 * depth) ** 0.25

    sm = jax.nn.softmax(hgrn_lb_logits.astype(F32), axis=0)
    lower_bounds = jnp.cumsum(sm, axis=0) - sm[0:1]

    xt = x
    sorted_init = None
    seq_rows = 256 * bn
    c_hg = (0, 1, 2, 3)
    c_su = 4
    c_rg = (5, 6)
    c_gate = 7
    n_exp = router_w.shape[2]
    buf_rows = _tile_buffer_rows(MOE_TILE, n_exp, TOP_K)
    per_blk = FFN_ROWS // SUB_ROWS
    max_sub = n * TOP_K // SUB_ROWS + (n // MOE_TILE) * n_exp + n_exp * (per_blk - 1)
    n_sorted = -(-max_sub // per_blk) * FFN_ROWS
    for l in range(depth):
        proj, xt = _in_proj(xt, w_in, l, 2048, w)
        ya = _hgrn2(proj, c_hg, lower_bounds[l], hgrn_norm_w[l], bn, seq_rows)
        mats = _s5_matrices(s5_a_re[l], s5_a_im[l], s5_b_re[l], s5_b_im[l],
                            s5_c_re[l], s5_c_im[l], s5_log_dt[l])
        yb = _s5(proj, c_su, mats, s5_d[l], s5_glu_w[l], s5_glu_b[l], bn, seq_rows // 2)
        yc = _rglru(proj, c_rg, rg_conv_w[l], rg_conv_b[l], rg_wa[l], rg_ba[l], rg_wx[l], rg_bx[l],
                    jax.nn.softplus(-rg_lambda[l].astype(F32)), bn, seq_rows)
        x1 = _merge(xt, ya, yb, yc, proj, c_gate, w_branch, b_gate[l], w_out,
                    ln1_g[l], ln1_b[l], l, alpha, 512)
        pos, wsel, seg = _router(x1, router_w[l].T, router_bias[l], MOE_TILE)
        sub_e, sub_dst, blk_e, n_used = _dispatch_plan(seg[:, :, 0], buf_rows, n_sorted)
        if sorted_init is None:
            sorted_init = jnp.zeros((n_sorted + buf_rows, d), BF16)
        xs = _dispatch(x1, pos, sorted_init, sub_e, sub_dst, buf_rows)
        ys = _experts(xs, blk_e, n_used, exp_w1, exp_w3, exp_w2, l)
        xt = _combine(x1, pos, wsel, ys, sub_e, sub_dst, sh_w1, sh_w3, sh_w2, ln2_g[l], ln2_b[l],
                      l, alpha, buf_rows, out_batch=bn if l == depth - 1 else 0)
        sorted_init = ys
    return xt
```

```python
import functools
import math

import numpy as np
import jax
import jax.numpy as jnp
from jax import lax
from jax.experimental import pallas as pl
from jax.experimental.pallas import tpu as pltpu

F32 = jnp.float32
BF16 = jnp.bfloat16

HG_HEADS = 4
HG_HEAD_DIM = 128
HG_CHUNK = 16
S5_GROUP_CH = 16
S5_STATE = 64
RG_BLOCK_W = 128
RG_CONV = 4
RG_C = 8.0
N_EXPERTS = 64
N_EXPERT_GROUPS = 8
TOPK_GROUPS = 4
TOP_K = 8
ROUTED_SCALE = 2.5
LN_EPS = 1e-5
RMS_EPS = 1e-6
LANES = 128
SUBLANES = 8
VMEM_LIMIT = 56 * 1024 * 1024
SUB_ROWS = 16
MOE_TILE = 512
FFN_ROWS = 512
MM_ROWS = 512


def _cparams(sem):
    return pltpu.CompilerParams(dimension_semantics=sem, vmem_limit_bytes=VMEM_LIMIT)


def _sigmoid(x):
    return 1.0 / (1.0 + jnp.exp(-x))


def _silu(x):
    return x * _sigmoid(x)


def _gelu_tanh(x):
    c = math.sqrt(2.0 / math.pi)
    return 0.5 * x * (1.0 + jnp.tanh(c * (x + 0.044715 * (x * x * x))))


def _layernorm(z, g, b):
    mu = jnp.mean(z, axis=-1, keepdims=True)
    zc = z - mu
    var = jnp.mean(zc * zc, axis=-1, keepdims=True)
    return zc * lax.rsqrt(var + LN_EPS) * g + b


def _dot(a, b):
    return jnp.dot(a, b, preferred_element_type=F32)


def _interleave_batches(x_ref, slab_ref):
    bn, t, c = x_ref.shape
    for b in range(bn):
        for sl in range(c // LANES):
            slab_ref[sl, pl.ds(b, t, stride=bn), :] = x_ref[b, :, sl * LANES:(sl + 1) * LANES]
    return jnp.concatenate([slab_ref[sl] for sl in range(c // LANES)], axis=1)


def _split_batches(y, o_ref, slab_ref):
    bn, t, c = o_ref.shape
    for sl in range(c // LANES):
        slab_ref[sl] = y[:, sl * LANES:(sl + 1) * LANES]
    for b in range(bn):
        for sl in range(c // LANES):
            o_ref[b, :, sl * LANES:(sl + 1) * LANES] = slab_ref[sl, pl.ds(b, t, stride=bn), :]


def _matmul_kernel(x_ref, w_ref, o_ref, xb_ref):
    @pl.when(pl.program_id(1) == 0)
    def _():
        xb_ref[...] = x_ref[...].astype(BF16)

    o_ref[...] = _dot(xb_ref[...], w_ref[0].astype(BF16)).astype(o_ref.dtype)


def _matmul_bsd_kernel(x_ref, w_ref, o_ref, xt_ref, xb_ref, slab_ref):
    @pl.when(pl.program_id(1) == 0)
    def _():
        xt = _interleave_batches(x_ref, slab_ref)
        xt_ref[...] = xt
        xb_ref[...] = xt.astype(BF16)

    o_ref[...] = _dot(xb_ref[...], w_ref[0].astype(BF16)).astype(o_ref.dtype)


def _in_proj(x, w, layer, tm, tn):
    k, n = w.shape[1], w.shape[2]
    w_spec = pl.BlockSpec((1, k, tn), lambda i, j: (layer, 0, j))
    sem = _cparams(("parallel", "arbitrary"))
    if x.ndim == 2:
        m = x.shape[0]
        return pl.pallas_call(
            _matmul_kernel,
            grid=(m // tm, n // tn),
            in_specs=[pl.BlockSpec((tm, k), lambda i, j: (i, 0)), w_spec],
            out_specs=pl.BlockSpec((tm, tn), lambda i, j: (i, j)),
            out_shape=jax.ShapeDtypeStruct((m, n), BF16),
            scratch_shapes=[pltpu.VMEM((tm, k), BF16)],
            compiler_params=sem,
            name="in_proj",
        )(x, w), x
    bn, s, _ = x.shape
    m = bn * s
    return pl.pallas_call(
        _matmul_bsd_kernel,
        grid=(m // tm, n // tn),
        in_specs=[pl.BlockSpec((bn, tm // bn, k), lambda i, j: (0, i, 0)), w_spec],
        out_specs=[pl.BlockSpec((tm, tn), lambda i, j: (i, j)),
                   pl.BlockSpec((tm, k), lambda i, j: (i, 0))],
        out_shape=[jax.ShapeDtypeStruct((m, n), BF16), jax.ShapeDtypeStruct((m, k), F32)],
        scratch_shapes=[pltpu.VMEM((tm, k), BF16), pltpu.VMEM((k // LANES, tm, LANES), F32)],
        compiler_params=sem,
        name="in_proj_bsd",
    )(x, w)


def _hgrn2_kernel(q_ref, f_ref, v_ref, g_ref, lb_ref, nw_ref, o_ref, st_ref, kvb_ref, *, batch):
    rows = q_ref.shape[0]
    cr = HG_CHUNK * batch
    n_chunks = rows // cr
    dh = HG_HEAD_DIM

    @pl.when(pl.program_id(0) == 0)
    def _():
        st_ref[...] = jnp.zeros_like(st_ref)
        kvb_ref[...] = jnp.zeros_like(kvb_ref)

    row = lax.broadcasted_iota(jnp.int32, (cr, dh), 0)
    row_b = row % batch
    ones_sum = jnp.ones((dh, dh), BF16)

    def chunk(c, carry):
        r0 = pl.multiple_of(c * cr, cr)
        for h in range(HG_HEADS):
            ls = slice(h * dh, (h + 1) * dh)
            lb = lb_ref[:, ls]
            f = lb + (1.0 - lb) * _sigmoid(f_ref[pl.ds(r0, cr), ls].astype(F32))
            q = _silu(q_ref[pl.ds(r0, cr), ls].astype(F32))
            k = 1.0 - f
            v = v_ref[pl.ds(r0, cr), ls].astype(F32)
            bc = jnp.log(f)
            sh = batch
            while sh < cr:
                bc = bc + jnp.where(row >= sh, pltpu.roll(bc, sh, 0), 0.0)
                sh *= 2
            p0 = (q * k).astype(BF16)
            o = _dot(p0, ones_sum) * v
            for idx, a in enumerate((k, v, bc)):
                kvb_ref[h, idx, SUBLANES:SUBLANES + cr, :] = a
            for j in range(1, HG_CHUNK):
                s = j * batch
                lo = SUBLANES * (s // SUBLANES)
                n = cr - lo
                start = SUBLANES - (s - lo)
                ks, vs, bs = (kvb_ref[h, idx, start:start + n, :] for idx in range(3))
                p = q[lo:cr] * ks * jnp.exp(bc[lo:cr] - bs)
                if s != lo:
                    p = jnp.where(row[0:n] >= batch, p, 0.0)
                upd = _dot(p.astype(BF16), ones_sum) * vs
                o = jnp.concatenate([o[0:lo], o[lo:cr] + upd], axis=0) if lo else o + upd
            b_last = bc[cr - batch:cr, :]
            qt = q * jnp.exp(bc)
            kt = k * jnp.exp(jnp.concatenate([b_last] * HG_CHUNK, axis=0) - bc)
            qm = jnp.concatenate([jnp.where(row_b == b, qt, 0.0) for b in range(batch)], axis=1)
            km = jnp.concatenate([jnp.where(row_b == b, kt, 0.0) for b in range(batch)], axis=1)
            st = st_ref[h]
            o = o + lax.dot_general(qm.astype(BF16), st.astype(BF16),
                                    (((1,), (1,)), ((), ())), preferred_element_type=F32)
            kv = lax.dot_general(v.astype(BF16), km.astype(BF16),
                                 (((0,), (0,)), ((), ())), preferred_element_type=F32)
            dec_all = jnp.concatenate([jnp.exp(b_last[b:b + 1, :]) for b in range(batch)], axis=1)
            st_ref[h] = st * dec_all + kv
            o = o * lax.rsqrt(jnp.mean(o * o, axis=-1, keepdims=True) + RMS_EPS)
            o_ref[pl.ds(r0, cr), ls] = o * nw_ref[:, ls] * _silu(g_ref[pl.ds(r0, cr), ls].astype(F32))
        return carry

    lax.fori_loop(0, n_chunks, chunk, 0)


def _hgrn2(proj, cols, lb, norm_w, batch, rows):
    assert 2 * batch == SUBLANES, "two time steps must fill the 8 sublanes"
    n = proj.shape[0]
    w = HG_HEADS * HG_HEAD_DIM
    cq, cf, cv, cg = cols

    def spec(cb):
        return pl.BlockSpec((rows, w), lambda i: (i, cb))

    vec = pl.BlockSpec((1, w), lambda i: (0, 0))
    return pl.pallas_call(
        functools.partial(_hgrn2_kernel, batch=batch),
        grid=(n // rows,),
        in_specs=[spec(cq), spec(cf), spec(cv), spec(cg), vec, vec],
        out_specs=pl.BlockSpec((rows, w), lambda i: (i, 0)),
        out_shape=jax.ShapeDtypeStruct((n, w), F32),
        scratch_shapes=[pltpu.VMEM((HG_HEADS, HG_HEAD_DIM, batch * HG_HEAD_DIM), F32),
                        pltpu.VMEM((HG_HEADS, 3, SUBLANES + HG_CHUNK * batch, HG_HEAD_DIM), F32)],
        compiler_params=_cparams(("arbitrary",)),
        name="hgrn2",
    )(proj, proj, proj, proj, lb.reshape(1, w), norm_w.reshape(1, w))


S5_TILE = 128
S5_MM_ROWS = 512


def _s5_kernel(u_ref, pin_ref, bm_ref, ar_ref, ai_ref, cm_ref, d_ref, gw_ref, gb_ref,
               o_ref, u2_ref, hs_ref, h_ref, ys_ref, *, batch):
    rows, w = u_ref.shape
    nblk = w // LANES
    groups = hs_ref.shape[0]
    gpb = groups // nblk
    tile2 = 2 * S5_TILE
    pair = 2 * batch
    steps = rows // batch

    @pl.when(pl.program_id(0) == 0)
    def _():
        h_ref[...] = jnp.zeros_like(h_ref)

    for t in range(rows // S5_TILE):
        ub = u_ref[t * S5_TILE:(t + 1) * S5_TILE, :].astype(BF16)
        up = _dot(pin_ref[...], ub).astype(BF16)
        u2_ref[t * tile2:(t + 1) * tile2, 0:w] = up[0:tile2]
        u2_ref[t * tile2:(t + 1) * tile2, w:2 * w] = up[tile2:2 * tile2]
    mm = min(S5_MM_ROWS, 2 * rows)
    for j in range(nblk):
        for c in range(2 * rows // mm):
            rs = slice(c * mm, (c + 1) * mm)
            lhs = jnp.concatenate([u2_ref[rs, j * LANES:(j + 1) * LANES],
                                   u2_ref[rs, w + j * LANES:w + (j + 1) * LANES]], axis=1)
            bu = _dot(lhs, bm_ref[j])
            for q in range(gpb):
                hs_ref[j * gpb + q, rs, :] = bu[:, q * LANES:(q + 1) * LANES]

    top = lax.broadcasted_iota(jnp.int32, (pair, LANES), 0) < batch
    per_pass = 8
    unroll = 4
    for p in range(groups // per_pass):
        gs = [p * per_pass + g for g in range(per_pass)]
        ar8 = [jnp.broadcast_to(ar_ref[:, g * LANES:(g + 1) * LANES], (pair, LANES)) for g in gs]
        ai8 = [jnp.where(top, -1.0, 1.0) * ai_ref[:, g * LANES:(g + 1) * LANES] for g in gs]

        def body(i, hs):
            hs = list(hs)
            for s in range(unroll):
                r0 = pl.multiple_of((i * unroll + s) * pair, pair)
                for k, g in enumerate(gs):
                    h = ar8[k] * hs[k] + ai8[k] * pltpu.roll(hs[k], batch, 0) + hs_ref[g, pl.ds(r0, pair), :]
                    hs_ref[g, pl.ds(r0, pair), :] = h
                    hs[k] = h
            return tuple(hs)

        h0 = tuple(h_ref[:, g * LANES:(g + 1) * LANES] for g in gs)
        hn = lax.fori_loop(0, steps // unroll, body, h0)
        for k, g in enumerate(gs):
            h_ref[:, g * LANES:(g + 1) * LANES] = hn[k]

    for b in range(batch):
        for j in range(nblk):
            parts = [hs_ref[j * gpb + q, pl.ds(off + b, steps, stride=pair), :]
                     for off in (0, batch) for q in range(gpb)]
            lhs = jnp.concatenate(parts, axis=1).astype(BF16)
            ys_ref[j, pl.ds(b, steps, stride=batch), :] = _dot(lhs, cm_ref[j])
    y = jnp.concatenate([ys_ref[j] for j in range(nblk)], axis=1) + d_ref[...] * u_ref[...].astype(F32)
    y = _gelu_tanh(y)
    z = _dot(y.astype(BF16), gw_ref[...].astype(BF16)) + gb_ref[...]
    o_ref[...] = y * _sigmoid(z)


def _s5_layout_matrix(batch):
    assert 2 * batch == SUBLANES, "one time step must fill the 8 sublanes"
    tile2 = 2 * S5_TILE
    r2 = np.arange(tile2)
    src = (r2 // (2 * batch)) * batch + r2 % batch
    is_re = (r2 % (2 * batch)) < batch
    onehot = (src[:, None] == np.arange(S5_TILE)[None, :])
    pin = np.concatenate([onehot & is_re[:, None], onehot & ~is_re[:, None]], axis=0)
    return jnp.asarray(pin.astype(np.float32), BF16)


def _s5(proj, col, mats, d, glu_w, glu_b, batch, rows):
    n = proj.shape[0]
    w = d.shape[0]
    bmat, ar, ai, cmat = mats
    nblk, _, cpb = bmat.shape
    nc = nblk * cpb
    pin = _s5_layout_matrix(batch)
    full = lambda a: pl.BlockSpec(a.shape, lambda i, nd=a.ndim: (0,) * nd)
    return pl.pallas_call(
        functools.partial(_s5_kernel, batch=batch),
        grid=(n // rows,),
        in_specs=[pl.BlockSpec((rows, w), lambda i: (i, col)),
                  full(pin), full(bmat), full(ar), full(ai), full(cmat),
                  pl.BlockSpec((1, w), lambda i: (0, 0)),
                  pl.BlockSpec((w, w), lambda i: (0, 0)),
                  pl.BlockSpec((1, w), lambda i: (0, 0))],
        out_specs=pl.BlockSpec((rows, w), lambda i: (i, 0)),
        out_shape=jax.ShapeDtypeStruct((n, w), F32),
        scratch_shapes=[pltpu.VMEM((2 * rows, 2 * w), BF16),
                        pltpu.VMEM((nc // LANES, 2 * rows, LANES), F32),
                        pltpu.VMEM((2 * batch, nc), F32),
                        pltpu.VMEM((w // LANES, rows, LANES), F32)],
        compiler_params=_cparams(("arbitrary",)),
        name="s5",
    )(proj, pin, bmat, ar, ai, cmat, d.reshape(1, w), glu_w, glu_b.reshape(1, w))


def _s5_matrices(a_re, a_im, b_re, b_im, c_re, c_im, log_dt):
    g, p = a_re.shape
    ch = b_re.shape[-1]
    gpb = LANES // ch
    nblk = g // gpb
    dt = jnp.exp(log_dt.astype(F32))[:, None]
    mag = jnp.exp(a_re * dt)
    abr = mag * jnp.cos(a_im * dt)
    abi = mag * jnp.sin(a_im * dt)
    den = a_re * a_re + a_im * a_im
    cr = ((abr - 1.0) * a_re + abi * a_im) / den
    ci = (abi * a_re - (abr - 1.0) * a_im) / den
    bbr = cr[..., None] * b_re - ci[..., None] * b_im
    bbi = cr[..., None] * b_im + ci[..., None] * b_re
    eye = jnp.eye(gpb, dtype=F32)

    def in_block(m):
        m = m.reshape(nblk, gpb, p, ch)
        return jnp.einsum('jgpc,gh->jgchp', m, eye).reshape(nblk, gpb * ch, gpb * p)

    def out_block(m):
        m = m.reshape(nblk, gpb, ch, p)
        return jnp.einsum('jgcp,gh->jgphc', m, eye).reshape(nblk, gpb * p, gpb * ch)

    bmat = jnp.concatenate([in_block(bbr), in_block(bbi)], axis=1).astype(BF16)
    cmat = jnp.concatenate([out_block(c_re), -out_block(c_im)], axis=1).astype(BF16)
    return bmat, abr.reshape(1, g * p), abi.reshape(1, g * p), cmat


def _rglru_kernel(xg_ref, xr_ref, cw_ref, cb_ref, wa_ref, ba_ref, wx_ref, bx_ref, sp_ref,
                  o_ref, xp_ref, a_ref, b_ref, h_ref, *, batch):
    rows, w = xr_ref.shape
    halo = SUBLANES * ((RG_CONV - 1) * batch // SUBLANES + 1)
    pair = 2 * batch

    @pl.when(pl.program_id(0) == 0)
    def _():
        xp_ref[0:halo, :] = jnp.zeros((halo, w), F32)
        h_ref[...] = jnp.zeros_like(h_ref)

    xr = xr_ref[...].astype(F32)
    xp_ref[halo:halo + rows, :] = xr
    xc = cb_ref[...] + cw_ref[RG_CONV - 1:RG_CONV, :] * xr
    for i in range(RG_CONV - 1):
        back = (RG_CONV - 1 - i) * batch
        xc = xc + cw_ref[i:i + 1, :] * xp_ref[halo - back:halo - back + rows, :]
    xp_ref[0:halo, :] = xr[rows - halo:rows, :]

    xcb = xc.astype(BF16)
    nb = w // RG_BLOCK_W
    ra, ri = [], []
    for hblk in range(nb):
        xs = xcb[:, hblk * RG_BLOCK_W:(hblk + 1) * RG_BLOCK_W]
        ra.append(_dot(xs, wa_ref[hblk].astype(BF16)))
        ri.append(_dot(xs, wx_ref[hblk].astype(BF16)))
    r = _sigmoid(jnp.concatenate(ra, axis=1) + ba_ref[...])
    ig = _sigmoid(jnp.concatenate(ri, axis=1) + bx_ref[...])
    log_a = (-RG_C) * r * sp_ref[...]
    a = jnp.exp(log_a)
    a_ref[...] = a
    b_ref[...] = jnp.sqrt(-jnp.tanh(log_a) * (a * a + 1.0)) * (ig * xc)

    top = lax.broadcasted_iota(jnp.int32, (pair, w), 0) < batch
    unroll = 4

    def body(i, h):
        for s in range(unroll):
            r0 = pl.multiple_of((i * unroll + s) * pair, pair)
            at = a_ref[pl.ds(r0, pair), :]
            bt = b_ref[pl.ds(r0, pair), :]
            n1 = at * h + bt
            h1 = jnp.where(top, n1, pltpu.roll(n1, batch, 0))
            n2 = at * h1 + bt
            b_ref[pl.ds(r0, pair), :] = jnp.where(top, n1, n2)
            h = jnp.where(top, pltpu.roll(n2, batch, 0), n2)
        return h

    h_ref[...] = lax.fori_loop(0, rows // (pair * unroll), body, h_ref[...])
    o_ref[...] = b_ref[...] * _gelu_tanh(xg_ref[...].astype(F32))


def _rglru(proj, cols, conv_w, conv_b, wa, ba, wx, bx, softplus_neg_lam, batch, rows):
    assert 2 * batch == SUBLANES, "two time steps must fill the 8 sublanes"
    n = proj.shape[0]
    w = conv_b.shape[0]
    cg, cx = cols
    nb = wa.shape[0]
    halo = SUBLANES * ((RG_CONV - 1) * batch // SUBLANES + 1)
    vec = pl.BlockSpec((1, w), lambda i: (0, 0))
    blk = pl.BlockSpec((nb, RG_BLOCK_W, RG_BLOCK_W), lambda i: (0, 0, 0))
    return pl.pallas_call(
        functools.partial(_rglru_kernel, batch=batch),
        grid=(n // rows,),
        in_specs=[pl.BlockSpec((rows, w), lambda i: (i, cg)),
                  pl.BlockSpec((rows, w), lambda i: (i, cx)),
                  pl.BlockSpec((RG_CONV, w), lambda i: (0, 0)),
                  vec, blk, vec, blk, vec, vec],
        out_specs=pl.BlockSpec((rows, w), lambda i: (i, 0)),
        out_shape=jax.ShapeDtypeStruct((n, w), F32),
        scratch_shapes=[pltpu.VMEM((halo + rows, w), F32),
                        pltpu.VMEM((rows, w), F32),
                        pltpu.VMEM((rows, w), F32),
                        pltpu.VMEM((2 * batch, w), F32)],
        compiler_params=_cparams(("arbitrary",)),
        name="rglru",
    )(proj, proj, conv_w, conv_b.reshape(1, w), wa, ba.reshape(1, w), wx, bx.reshape(1, w),
      softplus_neg_lam.reshape(1, w))


def _merge_kernel(x_ref, ya_ref, yb_ref, yc_ref, *rest, alpha, halves):
    n_gate = 3 * halves
    gate_refs = rest[:n_gate]
    wb_ref, bg_ref, wo_ref, lg_ref, lb_ref, o_ref = rest[n_gate:]
    m = None
    for kbr, y_ref in enumerate((ya_ref, yb_ref, yc_ref)):
        br = _dot(y_ref[...].astype(BF16), wb_ref[0, kbr].astype(BF16))
        gp = jnp.concatenate([gate_refs[kbr * halves + i][...] for i in range(halves)], axis=1)
        t = _sigmoid(gp.astype(F32) + bg_ref[kbr:kbr + 1, :]) * br
        m = t if m is None else m + t
    mix = _dot(m.astype(BF16), wo_ref[0].astype(BF16))
    o_ref[...] = _layernorm(alpha * x_ref[...] + mix, lg_ref[...], lb_ref[...])


def _merge(x, ya, yb, yc, proj, gate_col0, w_branch, b_gate, w_out, ln_g, ln_b, layer, alpha, tm):
    n, d = x.shape
    w = ya.shape[1]
    nbr = w_branch.shape[1]
    halves = d // w
    row = lambda width: pl.BlockSpec((tm, width), lambda i: (i, 0))
    gates = [pl.BlockSpec((tm, w), lambda i, c=gate_col0 + c: (i, c)) for c in range(nbr * halves)]
    vec = pl.BlockSpec((1, d), lambda i: (0, 0))
    return pl.pallas_call(
        functools.partial(_merge_kernel, alpha=alpha, halves=halves),
        grid=(n // tm,),
        in_specs=[row(d), row(w), row(w), row(w)] + gates + [
            pl.BlockSpec((1, nbr, w, d), lambda i: (layer, 0, 0, 0)),
            pl.BlockSpec((nbr, d), lambda i: (0, 0)),
            pl.BlockSpec((1, d, d), lambda i: (layer, 0, 0)), vec, vec],
        out_specs=row(d),
        out_shape=jax.ShapeDtypeStruct((n, d), F32),
        compiler_params=_cparams(("parallel",)),
        name="merge",
    )(x, ya, yb, yc, *([proj] * (nbr * halves)), w_branch, b_gate.reshape(nbr, d), w_out,
      ln_g.reshape(1, d), ln_b.reshape(1, d))


def _first_max_mask(cur, idx, axis):
    m = jnp.max(cur, axis=axis, keepdims=True)
    first = jnp.min(jnp.where(cur == m, idx, jnp.int32(2 ** 30)), axis=axis, keepdims=True)
    return idx == first


def _router_kernel(x_ref, wr_ref, rb_ref, pos_ref, w_ref, seg_ref):
    tm = x_ref.shape[0]
    e = N_EXPERTS
    per = e // N_EXPERT_GROUPS
    x = x_ref[...]
    wr = wr_ref[...]
    xh = x.astype(BF16)
    xl = (x - xh.astype(F32)).astype(BF16)
    wh = wr.astype(BF16)
    wl = (wr - wh.astype(F32)).astype(BF16)
    nt = (((1,), (1,)), ((), ()))
    logits = (lax.dot_general(wh, xh, nt, preferred_element_type=F32)
              + lax.dot_general(wh, xl, nt, preferred_element_type=F32)
              + lax.dot_general(wl, xh, nt, preferred_element_type=F32))
    scores = _sigmoid(logits)
    biased = scores + rb_ref[...]
    neg = jnp.float32(-jnp.inf)

    b3 = biased.reshape(N_EXPERT_GROUPS, per, tm)
    i3 = lax.broadcasted_iota(jnp.int32, b3.shape, 1)
    top1 = _first_max_mask(b3, i3, 1)
    m1 = jnp.max(b3, axis=1, keepdims=True)
    m2 = jnp.max(jnp.where(top1, neg, b3), axis=1, keepdims=True)
    gscore = (m1 + m2).reshape(N_EXPERT_GROUPS, tm)

    ig = lax.broadcasted_iota(jnp.int32, gscore.shape, 0)
    gsel = jnp.zeros(gscore.shape, F32)
    cur = gscore
    for _ in range(TOPK_GROUPS):
        pick = _first_max_mask(cur, ig, 0)
        gsel = jnp.where(pick, 1.0, gsel)
        cur = jnp.where(pick, neg, cur)

    gsel3 = jnp.broadcast_to(gsel.reshape(N_EXPERT_GROUPS, 1, tm), b3.shape)
    masked = jnp.where(gsel3 > 0.0, b3, neg).reshape(e, tm)
    ie = lax.broadcasted_iota(jnp.int32, masked.shape, 0)
    chosen = jnp.zeros(masked.shape, F32)
    cur = masked
    for _ in range(TOP_K):
        pick = _first_max_mask(cur, ie, 0)
        chosen = jnp.where(pick, 1.0, chosen)
        cur = jnp.where(pick, neg, cur)

    wsel = jnp.where(chosen > 0.0, scores, 0.0)
    wsel = wsel / jnp.sum(wsel, axis=0, keepdims=True) * ROUTED_SCALE
    w_ref[0] = wsel

    cb = chosen.astype(BF16)
    tok_r = lax.broadcasted_iota(jnp.int32, (tm, tm), 0)
    tok_c = lax.broadcasted_iota(jnp.int32, (tm, tm), 1)
    before = jnp.where(tok_r < tok_c, 1.0, 0.0).astype(BF16)
    prefix = _dot(cb, before)
    cnt = jnp.sum(chosen, axis=1, keepdims=True)
    seg = jnp.floor((cnt + (SUB_ROWS - 1)) * (1.0 / SUB_ROWS))
    seg_b = jnp.broadcast_to(seg, (e, LANES))
    ex_r = lax.broadcasted_iota(jnp.int32, (e, e), 0)
    ex_c = lax.broadcasted_iota(jnp.int32, (e, e), 1)
    earlier = jnp.where(ex_c < ex_r, 1.0, 0.0).astype(BF16)
    seg_off = _dot(earlier, seg_b.astype(BF16))
    seg_ref[0] = seg_b
    pos_ref[0] = jnp.where(chosen > 0.0, seg_off[:, 0:1] * SUB_ROWS + prefix, -1.0)


def _router(x, router_w_t, router_bias, tm):
    n, d = x.shape
    e = router_w_t.shape[0]
    nt = n // tm
    tile = lambda width: pl.BlockSpec((1, e, width), lambda i: (i, 0, 0))
    return pl.pallas_call(
        _router_kernel,
        grid=(nt,),
        in_specs=[pl.BlockSpec((tm, d), lambda i: (i, 0)),
                  pl.BlockSpec((e, d), lambda i: (0, 0)),
                  pl.BlockSpec((e, 1), lambda i: (0, 0))],
        out_specs=[tile(tm), tile(tm), tile(LANES)],
        out_shape=[jax.ShapeDtypeStruct((nt, e, tm), F32),
                   jax.ShapeDtypeStruct((nt, e, tm), F32),
                   jax.ShapeDtypeStruct((nt, e, LANES), F32)],
        compiler_params=_cparams(("parallel",)),
        name="router",
    )(x, router_w_t, router_bias.reshape(e, 1))


def _tile_buffer_rows(tile, n_experts, top_k):
    rows = tile * top_k + n_experts * (SUB_ROWS - 1)
    return -(-rows // MM_ROWS) * MM_ROWS


def _slot_matrix(j, e, pos_ref, val_ref, tile):
    e = jnp.maximum(e, 0)
    prow = pos_ref[0, pl.ds(e, 1), :] - jnp.asarray(j * SUB_ROWS).astype(F32)
    tgt = lax.broadcasted_iota(jnp.int32, (SUB_ROWS, tile), 0).astype(F32)
    if val_ref is None:
        return jnp.where(prow == tgt, 1.0, 0.0).astype(BF16)
    return jnp.where(prow == tgt, val_ref[0, pl.ds(e, 1), :], 0.0).astype(BF16)


LOOP_UNROLL = 8


def _wait_sub_blocks(src, dst, sem, n_sub):
    def wait(i, c):
        for _ in range(LOOP_UNROLL):
            pltpu.make_async_copy(src.at[pl.ds(0, SUB_ROWS)], dst.at[pl.ds(0, SUB_ROWS)], sem).wait()
        return c

    lax.fori_loop(0, n_sub // LOOP_UNROLL, wait, 0)


def _dispatch_kernel(sub_e_ref, sub_dst_ref, x_ref, pos_ref, init_ref, xs_hbm, m_ref, xs_ref, sem):
    del init_ref
    t = pl.program_id(0)
    tile = x_ref.shape[0]
    n_sub = m_ref.shape[0] // SUB_ROWS
    base = t * n_sub
    slot = t % 2

    def build(j, c):
        r0 = pl.multiple_of(j * SUB_ROWS, SUB_ROWS)
        m_ref[pl.ds(r0, SUB_ROWS), :] = _slot_matrix(j, sub_e_ref[base + j], pos_ref, None, tile)
        return c

    lax.fori_loop(0, n_sub, build, 0, unroll=LOOP_UNROLL)
    xb = x_ref[...].astype(BF16)
    for c in range(m_ref.shape[0] // MM_ROWS):
        rs = slice(c * MM_ROWS, (c + 1) * MM_ROWS)
        xs_ref[slot, rs, :] = _dot(m_ref[rs, :], xb).astype(BF16)

    def start(j, c):
        r0 = pl.multiple_of(j * SUB_ROWS, SUB_ROWS)
        d0 = pl.multiple_of(sub_dst_ref[base + j] * SUB_ROWS, SUB_ROWS)
        pltpu.make_async_copy(xs_ref.at[slot, pl.ds(r0, SUB_ROWS)], xs_hbm.at[pl.ds(d0, SUB_ROWS)],
                              sem.at[slot]).start()
        return c

    lax.fori_loop(0, n_sub, start, 0, unroll=LOOP_UNROLL)

    @pl.when(t > 0)
    def _():
        _wait_sub_blocks(xs_ref.at[1 - slot], xs_hbm, sem.at[1 - slot], n_sub)

    @pl.when(t == pl.num_programs(0) - 1)
    def _():
        _wait_sub_blocks(xs_ref.at[slot], xs_hbm, sem.at[slot], n_sub)


def _dispatch(x, pos, init, sub_e, sub_dst, buf_rows):
    nt, e, tile = pos.shape
    d = init.shape[1]
    return pl.pallas_call(
        _dispatch_kernel,
        grid_spec=pltpu.PrefetchScalarGridSpec(
            num_scalar_prefetch=2,
            grid=(nt,),
            in_specs=[pl.BlockSpec((tile, d), lambda i, *_: (i, 0)),
                      pl.BlockSpec((1, e, tile), lambda i, *_: (i, 0, 0)),
                      pl.BlockSpec(memory_space=pl.ANY)],
            out_specs=pl.BlockSpec(memory_space=pl.ANY),
            scratch_shapes=[pltpu.VMEM((buf_rows, tile), BF16),
                            pltpu.VMEM((2, buf_rows, d), BF16),
                            pltpu.SemaphoreType.DMA((2,))]),
        out_shape=jax.ShapeDtypeStruct(init.shape, BF16),
        input_output_aliases={4: 0},
        compiler_params=_cparams(("arbitrary",)),
        name="moe_dispatch",
    )(sub_e.reshape(-1), sub_dst.reshape(-1), x, pos, init)


def _experts_kernel(blk_e_ref, n_used_ref, xs_ref, w1_ref, w3_ref, w2_ref, ys_ref, w13_ref, w2b_ref):
    i = pl.program_id(0)
    hid = w1_ref.shape[-1]
    used = i < n_used_ref[0]
    new_expert = jnp.logical_or(i == 0, blk_e_ref[i] != blk_e_ref[jnp.maximum(i - 1, 0)])

    @pl.when(jnp.logical_and(used, new_expert))
    def _():
        w13_ref[:, 0:hid] = w1_ref[0, 0].astype(BF16)
        w13_ref[:, hid:2 * hid] = w3_ref[0, 0].astype(BF16)
        w2b_ref[...] = w2_ref[0, 0].astype(BF16)

    @pl.when(used)
    def _():
        h = _dot(xs_ref[...], w13_ref[...])
        hh = (_silu(h[:, 0:hid]) * h[:, hid:2 * hid]).astype(BF16)
        ys_ref[...] = _dot(hh, w2b_ref[...]).astype(ys_ref.dtype)


def _experts(xs, blk_e, n_used, w1, w3, w2, layer):
    rows, d = xs.shape
    hid = w1.shape[-1]
    nblk = rows // FFN_ROWS
    rowblk = pl.BlockSpec((FFN_ROWS, d), lambda i, be, nu: (jnp.minimum(i, nu[0] - 1), 0))
    return pl.pallas_call(
        _experts_kernel,
        grid_spec=pltpu.PrefetchScalarGridSpec(
            num_scalar_prefetch=2,
            grid=(nblk,),
            in_specs=[rowblk,
                      pl.BlockSpec((1, 1, d, hid), lambda i, be, nu: (layer, be[i], 0, 0)),
                      pl.BlockSpec((1, 1, d, hid), lambda i, be, nu: (layer, be[i], 0, 0)),
                      pl.BlockSpec((1, 1, hid, d), lambda i, be, nu: (layer, be[i], 0, 0))],
            out_specs=rowblk,
            scratch_shapes=[pltpu.VMEM((d, 2 * hid), BF16), pltpu.VMEM((hid, d), BF16)]),
        out_shape=jax.ShapeDtypeStruct((rows, d), BF16),
        input_output_aliases={2: 0},
        compiler_params=_cparams(("arbitrary",)),
        name="moe_experts",
    )(blk_e, n_used, xs, w1, w3, w2)


def _combine_kernel(sub_e_ref, sub_dst_ref, x_ref, pos_ref, w_ref, ys_hbm, s1_ref, s3_ref, s2_ref,
                    lg_ref, lb_ref, o_ref, m_ref, ys_ref, sem, *maybe_slab, alpha):
    t = pl.program_id(0)
    tile = x_ref.shape[0]
    n_sub = m_ref.shape[0] // SUB_ROWS
    base = t * n_sub

    def start(j, c):
        r0 = pl.multiple_of(j * SUB_ROWS, SUB_ROWS)
        d0 = pl.multiple_of(sub_dst_ref[base + j] * SUB_ROWS, SUB_ROWS)
        pltpu.make_async_copy(ys_hbm.at[pl.ds(d0, SUB_ROWS)], ys_ref.at[pl.ds(r0, SUB_ROWS)], sem).start()
        return c

    def build(j, c):
        r0 = pl.multiple_of(j * SUB_ROWS, SUB_ROWS)
        m_ref[pl.ds(r0, SUB_ROWS), :] = _slot_matrix(j, sub_e_ref[base + j], pos_ref, w_ref, tile)
        return c

    lax.fori_loop(0, n_sub, start, 0, unroll=LOOP_UNROLL)
    lax.fori_loop(0, n_sub, build, 0, unroll=LOOP_UNROLL)
    _wait_sub_blocks(ys_hbm, ys_ref, sem, n_sub)

    x = x_ref[...]
    xb = x.astype(BF16)
    g1 = _dot(xb, s1_ref[0].astype(BF16))
    g3 = _dot(xb, s3_ref[0].astype(BF16))
    ffn = _dot((_silu(g1) * g3).astype(BF16), s2_ref[0].astype(BF16))
    tn = (((0,), (0,)), ((), ()))
    for c in range(m_ref.shape[0] // MM_ROWS):
        rs = slice(c * MM_ROWS, (c + 1) * MM_ROWS)
        ffn = ffn + lax.dot_general(m_ref[rs, :], ys_ref[rs, :], tn, preferred_element_type=F32)
    y = _layernorm(alpha * x + ffn, lg_ref[...], lb_ref[...])
    if maybe_slab:
        _split_batches(y, o_ref, maybe_slab[0])
    else:
        o_ref[...] = y


def _combine(x, pos, wsel, ys, sub_e, sub_dst, s1, s3, s2, ln_g, ln_b, layer, alpha, buf_rows, out_batch):
    n, d = x.shape
    nt, e, tile = pos.shape
    hid = s1.shape[-1]
    vec = pl.BlockSpec((1, d), lambda i, *_: (0, 0))
    scratch = [pltpu.VMEM((buf_rows, tile), BF16), pltpu.VMEM((buf_rows, d), BF16),
               pltpu.SemaphoreType.DMA(())]
    if out_batch:
        out = jax.ShapeDtypeStruct((out_batch, n // out_batch, d), F32)
        out_spec = pl.BlockSpec((out_batch, tile // out_batch, d), lambda i, *_: (0, i, 0))
        scratch.append(pltpu.VMEM((d // LANES, tile, LANES), F32))
    else:
        out = jax.ShapeDtypeStruct((n, d), F32)
        out_spec = pl.BlockSpec((tile, d), lambda i, *_: (i, 0))
    return pl.pallas_call(
        functools.partial(_combine_kernel, alpha=alpha),
        grid_spec=pltpu.PrefetchScalarGridSpec(
            num_scalar_prefetch=2,
            grid=(nt,),
            in_specs=[pl.BlockSpec((tile, d), lambda i, *_: (i, 0)),
                      pl.BlockSpec((1, e, tile), lambda i, *_: (i, 0, 0)),
                      pl.BlockSpec((1, e, tile), lambda i, *_: (i, 0, 0)),
                      pl.BlockSpec(memory_space=pl.ANY),
                      pl.BlockSpec((1, d, hid), lambda i, *_: (layer, 0, 0)),
                      pl.BlockSpec((1, d, hid), lambda i, *_: (layer, 0, 0)),
                      pl.BlockSpec((1, hid, d), lambda i, *_: (layer, 0, 0)), vec, vec],
            out_specs=out_spec,
            scratch_shapes=scratch),
        out_shape=out,
        compiler_params=_cparams(("arbitrary",)),
        name="moe_combine",
    )(sub_e.reshape(-1), sub_dst.reshape(-1), x, pos, wsel, ys, s1, s3, s2,
      ln_g.reshape(1, d), ln_b.reshape(1, d))


def _dispatch_plan(seg, buf_rows, n_sorted_rows):
    nt, ne = seg.shape
    n_sub = buf_rows // SUB_ROWS
    per_blk = FFN_ROWS // SUB_ROWS
    seg = seg.astype(jnp.int32)
    seg_end = jnp.cumsum(seg, axis=1)
    seg_start = seg_end - seg
    exp_sub = jnp.sum(seg, axis=0)
    exp_blk = (exp_sub + per_blk - 1) // per_blk
    blk_end = jnp.cumsum(exp_blk)
    exp_start = (blk_end - exp_blk) * per_blk
    dst_start = exp_start[None, :] + jnp.cumsum(seg, axis=0) - seg
    j = jnp.arange(n_sub, dtype=jnp.int32)
    sub_e = jnp.sum((seg_end[:, None, :] <= j[None, :, None]).astype(jnp.int32), axis=-1)
    used = j[None, :] < seg_end[:, -1:]
    sub_ec = jnp.minimum(sub_e, ne - 1)
    pick = sub_ec[:, :, None] == jnp.arange(ne, dtype=jnp.int32)[None, None, :]
    dst = j[None, :] + jnp.sum(jnp.where(pick, (dst_start - seg_start)[:, None, :], 0), axis=-1)
    spare = n_sorted_rows // SUB_ROWS + (jnp.arange(nt, dtype=jnp.int32)[:, None] % 2) * n_sub + j[None, :]
    sub_dst = jnp.where(used, dst, spare).astype(jnp.int32)
    sub_e = jnp.where(used, sub_ec, -1).astype(jnp.int32)
    nblk = (n_sorted_rows + 2 * buf_rows) // FFN_ROWS
    i = jnp.arange(nblk, dtype=jnp.int32)
    blk_e = jnp.minimum(jnp.sum((blk_end[None, :] <= i[:, None]).astype(jnp.int32), axis=-1), ne - 1)
    return sub_e, sub_dst, blk_e, blk_end[-1:].astype(jnp.int32)


def kernel(x, w_in, b_gate, hgrn_lb_logits, hgrn_norm_w, s5_a_re, s5_a_im, s5_b_re, s5_b_im,
           s5_c_re, s5_c_im, s5_d, s5_log_dt, s5_glu_w, s5_glu_b, rg_conv_w, rg_conv_b,
           rg_wa, rg_ba, rg_wx, rg_bx, rg_lambda, w_branch, w_out, ln1_g, ln1_b,
           router_w, router_bias, exp_w1, exp_w3, exp_w2, sh_w1, sh_w3, sh_w2, ln2_g, ln2_b):
    bn, s, d = x.shape
    depth = w_in.shape[0]
    n = bn * s
    w = hgrn_norm_w.shape[1]
    alpha = (2 * depth) ** 0.25

    sm = jax.nn.softmax(hgrn_lb_logits.astype(F32), axis=0)
    lower_bounds = jnp.cumsum(sm, axis=0) - sm[0:1]

    xt = x
    sorted_init = None
    seq_rows = 256 * bn
    c_hg = (0, 1, 2, 3)
    c_su = 4
    c_rg = (5, 6)
    c_gate = 7
    n_exp = router_w.shape[2]
    buf_rows = _tile_buffer_rows(MOE_TILE, n_exp, TOP_K)
    per_blk = FFN_ROWS // SUB_ROWS
    max_sub = n * TOP_K // SUB_ROWS + (n // MOE_TILE) * n_exp + n_exp * (per_blk - 1)
    n_sorted = -(-max_sub // per_blk) * FFN_ROWS
    for l in range(depth):
        proj, xt = _in_proj(xt, w_in, l, 2048, w)
        ya = _hgrn2(proj, c_hg, lower_bounds[l], hgrn_norm_w[l], bn, seq_rows)
        mats = _s5_matrices(s5_a_re[l], s5_a_im[l], s5_b_re[l], s5_b_im[l],
                            s5_c_re[l], s5_c_im[l], s5_log_dt[l])
        yb = _s5(proj, c_su, mats, s5_d[l], s5_glu_w[l], s5_glu_b[l], bn, seq_rows)
        yc = _rglru(proj, c_rg, rg_conv_w[l], rg_conv_b[l], rg_wa[l], rg_ba[l], rg_wx[l], rg_bx[l],
                    jax.nn.softplus(-rg_lambda[l].astype(F32)), bn, seq_rows)
        x1 = _merge(xt, ya, yb, yc, proj, c_gate, w_branch, b_gate[l], w_out,
                    ln1_g[l], ln1_b[l], l, alpha, 512)
        pos, wsel, seg = _router(x1, router_w[l].T, router_bias[l], MOE_TILE)
        sub_e, sub_dst, blk_e, n_used = _dispatch_plan(seg[:, :, 0], buf_rows, n_sorted)
        if sorted_init is None:
            sorted_init = jnp.zeros((n_sorted + 2 * buf_rows, d), BF16)
        xs = _dispatch(x1, pos, sorted_init, sub_e, sub_dst, buf_rows)
        ys = _experts(xs, blk_e, n_used, exp_w1, exp_w3, exp_w2, l)
        xt = _combine(x1, pos, wsel, ys, sub_e, sub_dst, sh_w1, sh_w3, sh_w2, ln2_g[l], ln2_b[l],
                      l, alpha, buf_rows, out_batch=bn if l == depth - 1 else 0)
        sorted_init = ys
    return xt
```

```python
import functools
import math

import numpy as np
import jax
import jax.numpy as jnp
from jax import lax
from jax.experimental import pallas as pl
from jax.experimental.pallas import tpu as pltpu

F32 = jnp.float32
BF16 = jnp.bfloat16

HG_HEADS = 4
HG_HEAD_DIM = 128
HG_CHUNK = 16
S5_GROUP_CH = 16
S5_STATE = 64
RG_BLOCK_W = 128
RG_CONV = 4
RG_C = 8.0
N_EXPERTS = 64
N_EXPERT_GROUPS = 8
TOPK_GROUPS = 4
TOP_K = 8
ROUTED_SCALE = 2.5
LN_EPS = 1e-5
RMS_EPS = 1e-6
LANES = 128
SUBLANES = 8
VMEM_LIMIT = 56 * 1024 * 1024
SUB_ROWS = 16
MOE_TILE = 512
FFN_ROWS = 1024
MM_ROWS = 512


def _cparams(sem):
    return pltpu.CompilerParams(dimension_semantics=sem, vmem_limit_bytes=VMEM_LIMIT)


def _sigmoid(x):
    return 1.0 / (1.0 + jnp.exp(-x))


def _silu(x):
    return x * _sigmoid(x)


def _gelu_tanh(x):
    c = math.sqrt(2.0 / math.pi)
    return 0.5 * x * (1.0 + jnp.tanh(c * (x + 0.044715 * (x * x * x))))


def _layernorm(z, g, b):
    mu = jnp.mean(z, axis=-1, keepdims=True)
    zc = z - mu
    var = jnp.mean(zc * zc, axis=-1, keepdims=True)
    return zc * lax.rsqrt(var + LN_EPS) * g + b


def _dot(a, b):
    return jnp.dot(a, b, preferred_element_type=F32)


def _interleave_batches(x_ref, slab_ref):
    bn, t, c = x_ref.shape
    for b in range(bn):
        for sl in range(c // LANES):
            slab_ref[sl, pl.ds(b, t, stride=bn), :] = x_ref[b, :, sl * LANES:(sl + 1) * LANES]
    return jnp.concatenate([slab_ref[sl] for sl in range(c // LANES)], axis=1)


def _split_batches(y, o_ref, slab_ref):
    bn, t, c = o_ref.shape
    for sl in range(c // LANES):
        slab_ref[sl] = y[:, sl * LANES:(sl + 1) * LANES]
    for b in range(bn):
        for sl in range(c // LANES):
            o_ref[b, :, sl * LANES:(sl + 1) * LANES] = slab_ref[sl, pl.ds(b, t, stride=bn), :]


def _matmul_kernel(x_ref, w_ref, o_ref, xb_ref):
    @pl.when(pl.program_id(1) == 0)
    def _():
        xb_ref[...] = x_ref[...].astype(BF16)

    o_ref[...] = _dot(xb_ref[...], w_ref[0].astype(BF16)).astype(o_ref.dtype)


def _matmul_bsd_kernel(x_ref, w_ref, o_ref, xt_ref, xb_ref, slab_ref):
    @pl.when(pl.program_id(1) == 0)
    def _():
        xt = _interleave_batches(x_ref, slab_ref)
        xt_ref[...] = xt
        xb_ref[...] = xt.astype(BF16)

    o_ref[...] = _dot(xb_ref[...], w_ref[0].astype(BF16)).astype(o_ref.dtype)


def _in_proj(x, w, layer, tm, tn):
    k, n = w.shape[1], w.shape[2]
    w_spec = pl.BlockSpec((1, k, tn), lambda i, j: (layer, 0, j))
    sem = _cparams(("parallel", "arbitrary"))
    if x.ndim == 2:
        m = x.shape[0]
        return pl.pallas_call(
            _matmul_kernel,
            grid=(m // tm, n // tn),
            in_specs=[pl.BlockSpec((tm, k), lambda i, j: (i, 0)), w_spec],
            out_specs=pl.BlockSpec((tm, tn), lambda i, j: (i, j)),
            out_shape=jax.ShapeDtypeStruct((m, n), BF16),
            scratch_shapes=[pltpu.VMEM((tm, k), BF16)],
            compiler_params=sem,
            name="in_proj",
        )(x, w), x
    bn, s, _ = x.shape
    m = bn * s
    return pl.pallas_call(
        _matmul_bsd_kernel,
        grid=(m // tm, n // tn),
        in_specs=[pl.BlockSpec((bn, tm // bn, k), lambda i, j: (0, i, 0)), w_spec],
        out_specs=[pl.BlockSpec((tm, tn), lambda i, j: (i, j)),
                   pl.BlockSpec((tm, k), lambda i, j: (i, 0))],
        out_shape=[jax.ShapeDtypeStruct((m, n), BF16), jax.ShapeDtypeStruct((m, k), F32)],
        scratch_shapes=[pltpu.VMEM((tm, k), BF16), pltpu.VMEM((k // LANES, tm, LANES), F32)],
        compiler_params=sem,
        name="in_proj_bsd",
    )(x, w)


def _hgrn2_kernel(q_ref, f_ref, v_ref, g_ref, lb_ref, nw_ref, o_ref, st_ref, kvb_ref, *, batch):
    rows = q_ref.shape[0]
    cr = HG_CHUNK * batch
    n_chunks = rows // cr
    dh = HG_HEAD_DIM

    @pl.when(pl.program_id(0) == 0)
    def _():
        st_ref[...] = jnp.zeros_like(st_ref)
        kvb_ref[...] = jnp.zeros_like(kvb_ref)

    row = lax.broadcasted_iota(jnp.int32, (cr, dh), 0)
    row_b = row % batch
    ones_sum = jnp.ones((dh, dh), BF16)

    def chunk(c, carry):
        r0 = pl.multiple_of(c * cr, cr)
        for h in range(HG_HEADS):
            ls = slice(h * dh, (h + 1) * dh)
            lb = lb_ref[:, ls]
            f = lb + (1.0 - lb) * _sigmoid(f_ref[pl.ds(r0, cr), ls].astype(F32))
            q = _silu(q_ref[pl.ds(r0, cr), ls].astype(F32))
            k = 1.0 - f
            v = v_ref[pl.ds(r0, cr), ls].astype(F32)
            bc = jnp.log2(f)
            sh = batch
            while sh < cr:
                bc = bc + jnp.where(row >= sh, pltpu.roll(bc, sh, 0), 0.0)
                sh *= 2
            p0 = (q * k).astype(BF16)
            o = _dot(p0, ones_sum) * v
            for idx, a in enumerate((k, v, bc)):
                kvb_ref[h, idx, SUBLANES:SUBLANES + cr, :] = a
            for j in range(1, HG_CHUNK):
                s = j * batch
                lo = SUBLANES * (s // SUBLANES)
                n = cr - lo
                start = SUBLANES - (s - lo)
                ks, vs, bs = (kvb_ref[h, idx, start:start + n, :] for idx in range(3))
                p = q[lo:cr] * ks * jnp.exp2(bc[lo:cr] - bs)
                if s != lo:
                    p = jnp.where(row[0:n] >= batch, p, 0.0)
                upd = _dot(p.astype(BF16), ones_sum) * vs
                o = jnp.concatenate([o[0:lo], o[lo:cr] + upd], axis=0) if lo else o + upd
            b_last = bc[cr - batch:cr, :]
            qt = q * jnp.exp2(bc)
            kt = k * jnp.exp2(jnp.concatenate([b_last] * HG_CHUNK, axis=0) - bc)
            qm = jnp.concatenate([jnp.where(row_b == b, qt, 0.0) for b in range(batch)], axis=1)
            km = jnp.concatenate([jnp.where(row_b == b, kt, 0.0) for b in range(batch)], axis=1)
            st = st_ref[h]
            o = o + lax.dot_general(qm.astype(BF16), st.astype(BF16),
                                    (((1,), (1,)), ((), ())), preferred_element_type=F32)
            kv = lax.dot_general(v.astype(BF16), km.astype(BF16),
                                 (((0,), (0,)), ((), ())), preferred_element_type=F32)
            dec_all = jnp.concatenate([jnp.exp2(b_last[b:b + 1, :]) for b in range(batch)], axis=1)
            st_ref[h] = st * dec_all + kv
            o = o * lax.rsqrt(jnp.mean(o * o, axis=-1, keepdims=True) + RMS_EPS)
            o_ref[pl.ds(r0, cr), ls] = o * nw_ref[:, ls] * _silu(g_ref[pl.ds(r0, cr), ls].astype(F32))
        return carry

    lax.fori_loop(0, n_chunks, chunk, 0)


def _hgrn2(proj, cols, lb, norm_w, batch, rows):
    assert 2 * batch == SUBLANES, "two time steps must fill the 8 sublanes"
    n = proj.shape[0]
    w = HG_HEADS * HG_HEAD_DIM
    cq, cf, cv, cg = cols

    def spec(cb):
        return pl.BlockSpec((rows, w), lambda i: (i, cb))

    vec = pl.BlockSpec((1, w), lambda i: (0, 0))
    return pl.pallas_call(
        functools.partial(_hgrn2_kernel, batch=batch),
        grid=(n // rows,),
        in_specs=[spec(cq), spec(cf), spec(cv), spec(cg), vec, vec],
        out_specs=pl.BlockSpec((rows, w), lambda i: (i, 0)),
        out_shape=jax.ShapeDtypeStruct((n, w), F32),
        scratch_shapes=[pltpu.VMEM((HG_HEADS, HG_HEAD_DIM, batch * HG_HEAD_DIM), F32),
                        pltpu.VMEM((HG_HEADS, 3, SUBLANES + HG_CHUNK * batch, HG_HEAD_DIM), F32)],
        compiler_params=_cparams(("arbitrary",)),
        name="hgrn2",
    )(proj, proj, proj, proj, lb.reshape(1, w), norm_w.reshape(1, w))


S5_TILE = 128
S5_MM_ROWS = 512


def _s5_kernel(u_ref, pin_ref, bm_ref, ar_ref, ai_ref, cm_ref, d_ref, gw_ref, gb_ref,
               o_ref, u2_ref, hs_ref, h_ref, ys_ref, *, batch):
    rows, w = u_ref.shape
    nblk = w // LANES
    groups = hs_ref.shape[0]
    gpb = groups // nblk
    tile2 = 2 * S5_TILE
    pair = 2 * batch
    steps = rows // batch

    @pl.when(pl.program_id(0) == 0)
    def _():
        h_ref[...] = jnp.zeros_like(h_ref)

    for t in range(rows // S5_TILE):
        ub = u_ref[t * S5_TILE:(t + 1) * S5_TILE, :].astype(BF16)
        up = _dot(pin_ref[...], ub).astype(BF16)
        u2_ref[t * tile2:(t + 1) * tile2, 0:w] = up[0:tile2]
        u2_ref[t * tile2:(t + 1) * tile2, w:2 * w] = up[tile2:2 * tile2]
    mm = min(S5_MM_ROWS, 2 * rows)
    for j in range(nblk):
        for c in range(2 * rows // mm):
            rs = slice(c * mm, (c + 1) * mm)
            lhs = jnp.concatenate([u2_ref[rs, j * LANES:(j + 1) * LANES],
                                   u2_ref[rs, w + j * LANES:w + (j + 1) * LANES]], axis=1)
            bu = _dot(lhs, bm_ref[j])
            for q in range(gpb):
                hs_ref[j * gpb + q, rs, :] = bu[:, q * LANES:(q + 1) * LANES]

    top = lax.broadcasted_iota(jnp.int32, (pair, LANES), 0) < batch
    per_pass = 8
    unroll = 4
    for p in range(groups // per_pass):
        gs = [p * per_pass + g for g in range(per_pass)]
        ar8 = [jnp.broadcast_to(ar_ref[:, g * LANES:(g + 1) * LANES], (pair, LANES)) for g in gs]
        ai8 = [jnp.where(top, -1.0, 1.0) * ai_ref[:, g * LANES:(g + 1) * LANES] for g in gs]

        def body(i, hs):
            hs = list(hs)
            for s in range(unroll):
                r0 = pl.multiple_of((i * unroll + s) * pair, pair)
                for k, g in enumerate(gs):
                    h = ar8[k] * hs[k] + ai8[k] * pltpu.roll(hs[k], batch, 0) + hs_ref[g, pl.ds(r0, pair), :]
                    hs_ref[g, pl.ds(r0, pair), :] = h
                    hs[k] = h
            return tuple(hs)

        h0 = tuple(h_ref[:, g * LANES:(g + 1) * LANES] for g in gs)
        hn = lax.fori_loop(0, steps // unroll, body, h0)
        for k, g in enumerate(gs):
            h_ref[:, g * LANES:(g + 1) * LANES] = hn[k]

    for b in range(batch):
        for j in range(nblk):
            parts = [hs_ref[j * gpb + q, pl.ds(off + b, steps, stride=pair), :]
                     for off in (0, batch) for q in range(gpb)]
            lhs = jnp.concatenate(parts, axis=1).astype(BF16)
            ys_ref[j, pl.ds(b, steps, stride=batch), :] = _dot(lhs, cm_ref[j])
    y = jnp.concatenate([ys_ref[j] for j in range(nblk)], axis=1) + d_ref[...] * u_ref[...].astype(F32)
    y = _gelu_tanh(y)
    z = _dot(y.astype(BF16), gw_ref[...].astype(BF16)) + gb_ref[...]
    o_ref[...] = y * _sigmoid(z)


def _s5_layout_matrix(batch):
    assert 2 * batch == SUBLANES, "one time step must fill the 8 sublanes"
    tile2 = 2 * S5_TILE
    r2 = np.arange(tile2)
    src = (r2 // (2 * batch)) * batch + r2 % batch
    is_re = (r2 % (2 * batch)) < batch
    onehot = (src[:, None] == np.arange(S5_TILE)[None, :])
    pin = np.concatenate([onehot & is_re[:, None], onehot & ~is_re[:, None]], axis=0)
    return jnp.asarray(pin.astype(np.float32), BF16)


def _s5(proj, col, mats, d, glu_w, glu_b, batch, rows):
    n = proj.shape[0]
    w = d.shape[0]
    bmat, ar, ai, cmat = mats
    nblk, _, cpb = bmat.shape
    nc = nblk * cpb
    pin = _s5_layout_matrix(batch)
    full = lambda a: pl.BlockSpec(a.shape, lambda i, nd=a.ndim: (0,) * nd)
    return pl.pallas_call(
        functools.partial(_s5_kernel, batch=batch),
        grid=(n // rows,),
        in_specs=[pl.BlockSpec((rows, w), lambda i: (i, col)),
                  full(pin), full(bmat), full(ar), full(ai), full(cmat),
                  pl.BlockSpec((1, w), lambda i: (0, 0)),
                  pl.BlockSpec((w, w), lambda i: (0, 0)),
                  pl.BlockSpec((1, w), lambda i: (0, 0))],
        out_specs=pl.BlockSpec((rows, w), lambda i: (i, 0)),
        out_shape=jax.ShapeDtypeStruct((n, w), F32),
        scratch_shapes=[pltpu.VMEM((2 * rows, 2 * w), BF16),
                        pltpu.VMEM((nc // LANES, 2 * rows, LANES), F32),
                        pltpu.VMEM((2 * batch, nc), F32),
                        pltpu.VMEM((w // LANES, rows, LANES), F32)],
        compiler_params=_cparams(("arbitrary",)),
        name="s5",
    )(proj, pin, bmat, ar, ai, cmat, d.reshape(1, w), glu_w, glu_b.reshape(1, w))


def _s5_matrices(a_re, a_im, b_re, b_im, c_re, c_im, log_dt):
    g, p = a_re.shape
    ch = b_re.shape[-1]
    gpb = LANES // ch
    nblk = g // gpb
    dt = jnp.exp(log_dt.astype(F32))[:, None]
    mag = jnp.exp(a_re * dt)
    abr = mag * jnp.cos(a_im * dt)
    abi = mag * jnp.sin(a_im * dt)
    den = a_re * a_re + a_im * a_im
    cr = ((abr - 1.0) * a_re + abi * a_im) / den
    ci = (abi * a_re - (abr - 1.0) * a_im) / den
    bbr = cr[..., None] * b_re - ci[..., None] * b_im
    bbi = cr[..., None] * b_im + ci[..., None] * b_re
    eye = jnp.eye(gpb, dtype=F32)

    def in_block(m):
        m = m.reshape(nblk, gpb, p, ch)
        return jnp.einsum('jgpc,gh->jgchp', m, eye).reshape(nblk, gpb * ch, gpb * p)

    def out_block(m):
        m = m.reshape(nblk, gpb, ch, p)
        return jnp.einsum('jgcp,gh->jgphc', m, eye).reshape(nblk, gpb * p, gpb * ch)

    bmat = jnp.concatenate([in_block(bbr), in_block(bbi)], axis=1).astype(BF16)
    cmat = jnp.concatenate([out_block(c_re), -out_block(c_im)], axis=1).astype(BF16)
    return bmat, abr.reshape(1, g * p), abi.reshape(1, g * p), cmat


def _rglru_kernel(xg_ref, xr_ref, cw_ref, cb_ref, wa_ref, ba_ref, wx_ref, bx_ref, sp_ref,
                  o_ref, xp_ref, a_ref, b_ref, h_ref, *, batch):
    rows, w = xr_ref.shape
    halo = SUBLANES * ((RG_CONV - 1) * batch // SUBLANES + 1)
    pair = 2 * batch

    @pl.when(pl.program_id(0) == 0)
    def _():
        xp_ref[0:halo, :] = jnp.zeros((halo, w), F32)
        h_ref[...] = jnp.zeros_like(h_ref)

    xr = xr_ref[...].astype(F32)
    xp_ref[halo:halo + rows, :] = xr
    xc = cb_ref[...] + cw_ref[RG_CONV - 1:RG_CONV, :] * xr
    for i in range(RG_CONV - 1):
        back = (RG_CONV - 1 - i) * batch
        xc = xc + cw_ref[i:i + 1, :] * xp_ref[halo - back:halo - back + rows, :]
    xp_ref[0:halo, :] = xr[rows - halo:rows, :]

    xcb = xc.astype(BF16)
    nb = w // RG_BLOCK_W
    ra, ri = [], []
    for hblk in range(nb):
        xs = xcb[:, hblk * RG_BLOCK_W:(hblk + 1) * RG_BLOCK_W]
        ra.append(_dot(xs, wa_ref[hblk].astype(BF16)))
        ri.append(_dot(xs, wx_ref[hblk].astype(BF16)))
    r = _sigmoid(jnp.concatenate(ra, axis=1) + ba_ref[...])
    ig = _sigmoid(jnp.concatenate(ri, axis=1) + bx_ref[...])
    log_a = (-RG_C) * r * sp_ref[...]
    a = jnp.exp(log_a)
    a_ref[...] = a
    b_ref[...] = jnp.sqrt(-jnp.tanh(log_a) * (a * a + 1.0)) * (ig * xc)

    top = lax.broadcasted_iota(jnp.int32, (pair, w), 0) < batch
    unroll = 4

    def body(i, h):
        for s in range(unroll):
            r0 = pl.multiple_of((i * unroll + s) * pair, pair)
            at = a_ref[pl.ds(r0, pair), :]
            bt = b_ref[pl.ds(r0, pair), :]
            n1 = at * h + bt
            h1 = jnp.where(top, n1, pltpu.roll(n1, batch, 0))
            n2 = at * h1 + bt
            b_ref[pl.ds(r0, pair), :] = jnp.where(top, n1, n2)
            h = jnp.where(top, pltpu.roll(n2, batch, 0), n2)
        return h

    h_ref[...] = lax.fori_loop(0, rows // (pair * unroll), body, h_ref[...])
    o_ref[...] = b_ref[...] * _gelu_tanh(xg_ref[...].astype(F32))


def _rglru(proj, cols, conv_w, conv_b, wa, ba, wx, bx, softplus_neg_lam, batch, rows):
    assert 2 * batch == SUBLANES, "two time steps must fill the 8 sublanes"
    n = proj.shape[0]
    w = conv_b.shape[0]
    cg, cx = cols
    nb = wa.shape[0]
    halo = SUBLANES * ((RG_CONV - 1) * batch // SUBLANES + 1)
    vec = pl.BlockSpec((1, w), lambda i: (0, 0))
    blk = pl.BlockSpec((nb, RG_BLOCK_W, RG_BLOCK_W), lambda i: (0, 0, 0))
    return pl.pallas_call(
        functools.partial(_rglru_kernel, batch=batch),
        grid=(n // rows,),
        in_specs=[pl.BlockSpec((rows, w), lambda i: (i, cg)),
                  pl.BlockSpec((rows, w), lambda i: (i, cx)),
                  pl.BlockSpec((RG_CONV, w), lambda i: (0, 0)),
                  vec, blk, vec, blk, vec, vec],
        out_specs=pl.BlockSpec((rows, w), lambda i: (i, 0)),
        out_shape=jax.ShapeDtypeStruct((n, w), F32),
        scratch_shapes=[pltpu.VMEM((halo + rows, w), F32),
                        pltpu.VMEM((rows, w), F32),
                        pltpu.VMEM((rows, w), F32),
                        pltpu.VMEM((2 * batch, w), F32)],
        compiler_params=_cparams(("arbitrary",)),
        name="rglru",
    )(proj, proj, conv_w, conv_b.reshape(1, w), wa, ba.reshape(1, w), wx, bx.reshape(1, w),
      softplus_neg_lam.reshape(1, w))


def _merge_kernel(x_ref, ya_ref, yb_ref, yc_ref, *rest, alpha, halves):
    n_gate = 3 * halves
    gate_refs = rest[:n_gate]
    wb_ref, bg_ref, wo_ref, lg_ref, lb_ref, o_ref = rest[n_gate:]
    m = None
    for kbr, y_ref in enumerate((ya_ref, yb_ref, yc_ref)):
        br = _dot(y_ref[...].astype(BF16), wb_ref[0, kbr].astype(BF16))
        gp = jnp.concatenate([gate_refs[kbr * halves + i][...] for i in range(halves)], axis=1)
        t = _sigmoid(gp.astype(F32) + bg_ref[kbr:kbr + 1, :]) * br
        m = t if m is None else m + t
    mix = _dot(m.astype(BF16), wo_ref[0].astype(BF16))
    o_ref[...] = _layernorm(alpha * x_ref[...] + mix, lg_ref[...], lb_ref[...])


def _merge(x, ya, yb, yc, proj, gate_col0, w_branch, b_gate, w_out, ln_g, ln_b, layer, alpha, tm):
    n, d = x.shape
    w = ya.shape[1]
    nbr = w_branch.shape[1]
    halves = d // w
    row = lambda width: pl.BlockSpec((tm, width), lambda i: (i, 0))
    gates = [pl.BlockSpec((tm, w), lambda i, c=gate_col0 + c: (i, c)) for c in range(nbr * halves)]
    vec = pl.BlockSpec((1, d), lambda i: (0, 0))
    return pl.pallas_call(
        functools.partial(_merge_kernel, alpha=alpha, halves=halves),
        grid=(n // tm,),
        in_specs=[row(d), row(w), row(w), row(w)] + gates + [
            pl.BlockSpec((1, nbr, w, d), lambda i: (layer, 0, 0, 0)),
            pl.BlockSpec((nbr, d), lambda i: (0, 0)),
            pl.BlockSpec((1, d, d), lambda i: (layer, 0, 0)), vec, vec],
        out_specs=row(d),
        out_shape=jax.ShapeDtypeStruct((n, d), F32),
        compiler_params=_cparams(("parallel",)),
        name="merge",
    )(x, ya, yb, yc, *([proj] * (nbr * halves)), w_branch, b_gate.reshape(nbr, d), w_out,
      ln_g.reshape(1, d), ln_b.reshape(1, d))


def _first_max_mask(cur, idx, axis):
    m = jnp.max(cur, axis=axis, keepdims=True)
    first = jnp.min(jnp.where(cur == m, idx, jnp.int32(2 ** 30)), axis=axis, keepdims=True)
    return idx == first


def _router_kernel(x_ref, wr_ref, rb_ref, pos_ref, w_ref, seg_ref):
    tm = x_ref.shape[0]
    e = N_EXPERTS
    per = e // N_EXPERT_GROUPS
    x = x_ref[...]
    wr = wr_ref[...]
    xh = x.astype(BF16)
    xl = (x - xh.astype(F32)).astype(BF16)
    wh = wr.astype(BF16)
    wl = (wr - wh.astype(F32)).astype(BF16)
    nt = (((1,), (1,)), ((), ()))
    logits = (lax.dot_general(wh, xh, nt, preferred_element_type=F32)
              + lax.dot_general(wh, xl, nt, preferred_element_type=F32)
              + lax.dot_general(wl, xh, nt, preferred_element_type=F32))
    scores = _sigmoid(logits)
    biased = scores + rb_ref[...]
    neg = jnp.float32(-jnp.inf)

    b3 = biased.reshape(N_EXPERT_GROUPS, per, tm)
    i3 = lax.broadcasted_iota(jnp.int32, b3.shape, 1)
    top1 = _first_max_mask(b3, i3, 1)
    m1 = jnp.max(b3, axis=1, keepdims=True)
    m2 = jnp.max(jnp.where(top1, neg, b3), axis=1, keepdims=True)
    gscore = (m1 + m2).reshape(N_EXPERT_GROUPS, tm)

    ig = lax.broadcasted_iota(jnp.int32, gscore.shape, 0)
    gsel = jnp.zeros(gscore.shape, F32)
    cur = gscore
    for _ in range(TOPK_GROUPS):
        pick = _first_max_mask(cur, ig, 0)
        gsel = jnp.where(pick, 1.0, gsel)
        cur = jnp.where(pick, neg, cur)

    gsel3 = jnp.broadcast_to(gsel.reshape(N_EXPERT_GROUPS, 1, tm), b3.shape)
    masked = jnp.where(gsel3 > 0.0, b3, neg).reshape(e, tm)
    ie = lax.broadcasted_iota(jnp.int32, masked.shape, 0)
    chosen = jnp.zeros(masked.shape, F32)
    cur = masked
    for _ in range(TOP_K):
        pick = _first_max_mask(cur, ie, 0)
        chosen = jnp.where(pick, 1.0, chosen)
        cur = jnp.where(pick, neg, cur)

    wsel = jnp.where(chosen > 0.0, scores, 0.0)
    wsel = wsel / jnp.sum(wsel, axis=0, keepdims=True) * ROUTED_SCALE
    w_ref[0] = wsel

    cb = chosen.astype(BF16)
    tok_r = lax.broadcasted_iota(jnp.int32, (tm, tm), 0)
    tok_c = lax.broadcasted_iota(jnp.int32, (tm, tm), 1)
    before = jnp.where(tok_r < tok_c, 1.0, 0.0).astype(BF16)
    prefix = _dot(cb, before)
    cnt = jnp.sum(chosen, axis=1, keepdims=True)
    seg = jnp.floor((cnt + (SUB_ROWS - 1)) * (1.0 / SUB_ROWS))
    seg_b = jnp.broadcast_to(seg, (e, LANES))
    ex_r = lax.broadcasted_iota(jnp.int32, (e, e), 0)
    ex_c = lax.broadcasted_iota(jnp.int32, (e, e), 1)
    earlier = jnp.where(ex_c < ex_r, 1.0, 0.0).astype(BF16)
    seg_off = _dot(earlier, seg_b.astype(BF16))
    seg_ref[0] = seg_b
    pos_ref[0] = jnp.where(chosen > 0.0, seg_off[:, 0:1] * SUB_ROWS + prefix, -1.0)


def _router(x, router_w_t, router_bias, tm):
    n, d = x.shape
    e = router_w_t.shape[0]
    nt = n // tm
    tile = lambda width: pl.BlockSpec((1, e, width), lambda i: (i, 0, 0))
    return pl.pallas_call(
        _router_kernel,
        grid=(nt,),
        in_specs=[pl.BlockSpec((tm, d), lambda i: (i, 0)),
                  pl.BlockSpec((e, d), lambda i: (0, 0)),
                  pl.BlockSpec((e, 1), lambda i: (0, 0))],
        out_specs=[tile(tm), tile(tm), tile(LANES)],
        out_shape=[jax.ShapeDtypeStruct((nt, e, tm), F32),
                   jax.ShapeDtypeStruct((nt, e, tm), F32),
                   jax.ShapeDtypeStruct((nt, e, LANES), F32)],
        compiler_params=_cparams(("parallel",)),
        name="router",
    )(x, router_w_t, router_bias.reshape(e, 1))


def _tile_buffer_rows(tile, n_experts, top_k):
    rows = tile * top_k + n_experts * (SUB_ROWS - 1)
    return -(-rows // MM_ROWS) * MM_ROWS


def _slot_matrix(j, e, pos_ref, val_ref, tile):
    e = jnp.maximum(e, 0)
    prow = pos_ref[0, pl.ds(e, 1), :] - jnp.asarray(j * SUB_ROWS).astype(F32)
    tgt = lax.broadcasted_iota(jnp.int32, (SUB_ROWS, tile), 0).astype(F32)
    if val_ref is None:
        return jnp.where(prow == tgt, 1.0, 0.0).astype(BF16)
    return jnp.where(prow == tgt, val_ref[0, pl.ds(e, 1), :], 0.0).astype(BF16)


LOOP_UNROLL = 8


def _wait_sub_blocks(src, dst, sem, n_sub):
    def wait(i, c):
        for _ in range(LOOP_UNROLL):
            pltpu.make_async_copy(src.at[pl.ds(0, SUB_ROWS)], dst.at[pl.ds(0, SUB_ROWS)], sem).wait()
        return c

    lax.fori_loop(0, n_sub // LOOP_UNROLL, wait, 0)


def _dispatch_kernel(sub_e_ref, sub_dst_ref, x_ref, pos_ref, init_ref, xs_hbm, m_ref, xs_ref, sem):
    del init_ref
    t = pl.program_id(0)
    tile = x_ref.shape[0]
    n_sub = m_ref.shape[0] // SUB_ROWS
    base = t * n_sub
    slot = t % 2

    def build(j, c):
        r0 = pl.multiple_of(j * SUB_ROWS, SUB_ROWS)
        m_ref[pl.ds(r0, SUB_ROWS), :] = _slot_matrix(j, sub_e_ref[base + j], pos_ref, None, tile)
        return c

    lax.fori_loop(0, n_sub, build, 0, unroll=LOOP_UNROLL)
    xb = x_ref[...].astype(BF16)
    for c in range(m_ref.shape[0] // MM_ROWS):
        rs = slice(c * MM_ROWS, (c + 1) * MM_ROWS)
        xs_ref[slot, rs, :] = _dot(m_ref[rs, :], xb).astype(BF16)

    def start(j, c):
        r0 = pl.multiple_of(j * SUB_ROWS, SUB_ROWS)
        d0 = pl.multiple_of(sub_dst_ref[base + j] * SUB_ROWS, SUB_ROWS)
        pltpu.make_async_copy(xs_ref.at[slot, pl.ds(r0, SUB_ROWS)], xs_hbm.at[pl.ds(d0, SUB_ROWS)],
                              sem.at[slot]).start()
        return c

    lax.fori_loop(0, n_sub, start, 0, unroll=LOOP_UNROLL)

    @pl.when(t > 0)
    def _():
        _wait_sub_blocks(xs_ref.at[1 - slot], xs_hbm, sem.at[1 - slot], n_sub)

    @pl.when(t == pl.num_programs(0) - 1)
    def _():
        _wait_sub_blocks(xs_ref.at[slot], xs_hbm, sem.at[slot], n_sub)


def _dispatch(x, pos, init, sub_e, sub_dst, buf_rows):
    nt, e, tile = pos.shape
    d = init.shape[1]
    return pl.pallas_call(
        _dispatch_kernel,
        grid_spec=pltpu.PrefetchScalarGridSpec(
            num_scalar_prefetch=2,
            grid=(nt,),
            in_specs=[pl.BlockSpec((tile, d), lambda i, *_: (i, 0)),
                      pl.BlockSpec((1, e, tile), lambda i, *_: (i, 0, 0)),
                      pl.BlockSpec(memory_space=pl.ANY)],
            out_specs=pl.BlockSpec(memory_space=pl.ANY),
            scratch_shapes=[pltpu.VMEM((buf_rows, tile), BF16),
                            pltpu.VMEM((2, buf_rows, d), BF16),
                            pltpu.SemaphoreType.DMA((2,))]),
        out_shape=jax.ShapeDtypeStruct(init.shape, BF16),
        input_output_aliases={4: 0},
        compiler_params=_cparams(("arbitrary",)),
        name="moe_dispatch",
    )(sub_e.reshape(-1), sub_dst.reshape(-1), x, pos, init)


def _experts_kernel(blk_e_ref, n_used_ref, xs_ref, w1_ref, w3_ref, w2_ref, ys_ref, w13_ref, w2b_ref):
    i = pl.program_id(0)
    hid = w1_ref.shape[-1]
    used = i < n_used_ref[0]
    new_expert = jnp.logical_or(i == 0, blk_e_ref[i] != blk_e_ref[jnp.maximum(i - 1, 0)])

    @pl.when(jnp.logical_and(used, new_expert))
    def _():
        w13_ref[:, 0:hid] = w1_ref[0, 0].astype(BF16)
        w13_ref[:, hid:2 * hid] = w3_ref[0, 0].astype(BF16)
        w2b_ref[...] = w2_ref[0, 0].astype(BF16)

    @pl.when(used)
    def _():
        h = _dot(xs_ref[...], w13_ref[...])
        hh = (_silu(h[:, 0:hid]) * h[:, hid:2 * hid]).astype(BF16)
        ys_ref[...] = _dot(hh, w2b_ref[...]).astype(ys_ref.dtype)


def _experts(xs, blk_e, n_used, w1, w3, w2, layer):
    rows, d = xs.shape
    hid = w1.shape[-1]
    nblk = rows // FFN_ROWS
    rowblk = pl.BlockSpec((FFN_ROWS, d), lambda i, be, nu: (jnp.minimum(i, nu[0] - 1), 0))
    return pl.pallas_call(
        _experts_kernel,
        grid_spec=pltpu.PrefetchScalarGridSpec(
            num_scalar_prefetch=2,
            grid=(nblk,),
            in_specs=[rowblk,
                      pl.BlockSpec((1, 1, d, hid), lambda i, be, nu: (layer, be[i], 0, 0)),
                      pl.BlockSpec((1, 1, d, hid), lambda i, be, nu: (layer, be[i], 0, 0)),
                      pl.BlockSpec((1, 1, hid, d), lambda i, be, nu: (layer, be[i], 0, 0))],
            out_specs=rowblk,
            scratch_shapes=[pltpu.VMEM((d, 2 * hid), BF16), pltpu.VMEM((hid, d), BF16)]),
        out_shape=jax.ShapeDtypeStruct((rows, d), BF16),
        input_output_aliases={2: 0},
        compiler_params=_cparams(("arbitrary",)),
        name="moe_experts",
    )(blk_e, n_used, xs, w1, w3, w2)


def _combine_kernel(sub_e_ref, sub_dst_ref, x_ref, pos_ref, w_ref, ys_hbm, s1_ref, s3_ref, s2_ref,
                    lg_ref, lb_ref, o_ref, m_ref, ys_ref, sem, *maybe_slab, alpha):
    t = pl.program_id(0)
    tile = x_ref.shape[0]
    n_sub = m_ref.shape[0] // SUB_ROWS
    base = t * n_sub
    slot = t % 2
    last = t == pl.num_programs(0) - 1

    def start(j, tile_base, sl):
        r0 = pl.multiple_of(j * SUB_ROWS, SUB_ROWS)
        d0 = pl.multiple_of(sub_dst_ref[tile_base + j] * SUB_ROWS, SUB_ROWS)
        pltpu.make_async_copy(ys_hbm.at[pl.ds(d0, SUB_ROWS)], ys_ref.at[sl, pl.ds(r0, SUB_ROWS)],
                              sem.at[sl]).start()

    def build(j, c):
        r0 = pl.multiple_of(j * SUB_ROWS, SUB_ROWS)
        m_ref[pl.ds(r0, SUB_ROWS), :] = _slot_matrix(j, sub_e_ref[base + j], pos_ref, w_ref, tile)
        return c

    def start_own(j, c):
        start(j, base, slot)
        return c

    def start_next(j, c):
        start(j, base + n_sub, 1 - slot)
        return c

    @pl.when(t == 0)
    def _():
        lax.fori_loop(0, n_sub, start_own, 0, unroll=LOOP_UNROLL)

    @pl.when(jnp.logical_not(last))
    def _():
        lax.fori_loop(0, n_sub, start_next, 0, unroll=LOOP_UNROLL)

    lax.fori_loop(0, n_sub, build, 0, unroll=LOOP_UNROLL)
    _wait_sub_blocks(ys_hbm, ys_ref.at[slot], sem.at[slot], n_sub)

    x = x_ref[...]
    xb = x.astype(BF16)
    g1 = _dot(xb, s1_ref[0].astype(BF16))
    g3 = _dot(xb, s3_ref[0].astype(BF16))
    ffn = _dot((_silu(g1) * g3).astype(BF16), s2_ref[0].astype(BF16))
    tn = (((0,), (0,)), ((), ()))
    for c in range(m_ref.shape[0] // MM_ROWS):
        rs = slice(c * MM_ROWS, (c + 1) * MM_ROWS)
        ffn = ffn + lax.dot_general(m_ref[rs, :], ys_ref[slot, rs, :], tn, preferred_element_type=F32)
    y = _layernorm(alpha * x + ffn, lg_ref[...], lb_ref[...])
    if maybe_slab:
        _split_batches(y, o_ref, maybe_slab[0])
    else:
        o_ref[...] = y


def _combine(x, pos, wsel, ys, sub_e, sub_dst, s1, s3, s2, ln_g, ln_b, layer, alpha, buf_rows, out_batch):
    n, d = x.shape
    nt, e, tile = pos.shape
    hid = s1.shape[-1]
    vec = pl.BlockSpec((1, d), lambda i, *_: (0, 0))
    scratch = [pltpu.VMEM((buf_rows, tile), BF16), pltpu.VMEM((2, buf_rows, d), BF16),
               pltpu.SemaphoreType.DMA((2,))]
    if out_batch:
        out = jax.ShapeDtypeStruct((out_batch, n // out_batch, d), F32)
        out_spec = pl.BlockSpec((out_batch, tile // out_batch, d), lambda i, *_: (0, i, 0))
        scratch.append(pltpu.VMEM((d // LANES, tile, LANES), F32))
    else:
        out = jax.ShapeDtypeStruct((n, d), F32)
        out_spec = pl.BlockSpec((tile, d), lambda i, *_: (i, 0))
    return pl.pallas_call(
        functools.partial(_combine_kernel, alpha=alpha),
        grid_spec=pltpu.PrefetchScalarGridSpec(
            num_scalar_prefetch=2,
            grid=(nt,),
            in_specs=[pl.BlockSpec((tile, d), lambda i, *_: (i, 0)),
                      pl.BlockSpec((1, e, tile), lambda i, *_: (i, 0, 0)),
                      pl.BlockSpec((1, e, tile), lambda i, *_: (i, 0, 0)),
                      pl.BlockSpec(memory_space=pl.ANY),
                      pl.BlockSpec((1, d, hid), lambda i, *_: (layer, 0, 0)),
                      pl.BlockSpec((1, d, hid), lambda i, *_: (layer, 0, 0)),
                      pl.BlockSpec((1, hid, d), lambda i, *_: (layer, 0, 0)), vec, vec],
            out_specs=out_spec,
            scratch_shapes=scratch),
        out_shape=out,
        compiler_params=_cparams(("arbitrary",)),
        name="moe_combine",
    )(sub_e.reshape(-1), sub_dst.reshape(-1), x, pos, wsel, ys, s1, s3, s2,
      ln_g.reshape(1, d), ln_b.reshape(1, d))


def _dispatch_plan(seg, buf_rows, n_sorted_rows):
    nt, ne = seg.shape
    n_sub = buf_rows // SUB_ROWS
    per_blk = FFN_ROWS // SUB_ROWS
    seg = seg.astype(jnp.int32)
    seg_end = jnp.cumsum(seg, axis=1)
    seg_start = seg_end - seg
    exp_sub = jnp.sum(seg, axis=0)
    exp_blk = (exp_sub + per_blk - 1) // per_blk
    blk_end = jnp.cumsum(exp_blk)
    exp_start = (blk_end - exp_blk) * per_blk
    dst_start = exp_start[None, :] + jnp.cumsum(seg, axis=0) - seg
    j = jnp.arange(n_sub, dtype=jnp.int32)
    sub_e = jnp.sum((seg_end[:, None, :] <= j[None, :, None]).astype(jnp.int32), axis=-1)
    used = j[None, :] < seg_end[:, -1:]
    sub_ec = jnp.minimum(sub_e, ne - 1)
    pick = sub_ec[:, :, None] == jnp.arange(ne, dtype=jnp.int32)[None, None, :]
    dst = j[None, :] + jnp.sum(jnp.where(pick, (dst_start - seg_start)[:, None, :], 0), axis=-1)
    spare = n_sorted_rows // SUB_ROWS + (jnp.arange(nt, dtype=jnp.int32)[:, None] % 2) * n_sub + j[None, :]
    sub_dst = jnp.where(used, dst, spare).astype(jnp.int32)
    sub_e = jnp.where(used, sub_ec, -1).astype(jnp.int32)
    nblk = (n_sorted_rows + 2 * buf_rows) // FFN_ROWS
    i = jnp.arange(nblk, dtype=jnp.int32)
    blk_e = jnp.minimum(jnp.sum((blk_end[None, :] <= i[:, None]).astype(jnp.int32), axis=-1), ne - 1)
    return sub_e, sub_dst, blk_e, blk_end[-1:].astype(jnp.int32)


def kernel(x, w_in, b_gate, hgrn_lb_logits, hgrn_norm_w, s5_a_re, s5_a_im, s5_b_re, s5_b_im,
           s5_c_re, s5_c_im, s5_d, s5_log_dt, s5_glu_w, s5_glu_b, rg_conv_w, rg_conv_b,
           rg_wa, rg_ba, rg_wx, rg_bx, rg_lambda, w_branch, w_out, ln1_g, ln1_b,
           router_w, router_bias, exp_w1, exp_w3, exp_w2, sh_w1, sh_w3, sh_w2, ln2_g, ln2_b):
    bn, s, d = x.shape
    depth = w_in.shape[0]
    n = bn * s
    w = hgrn_norm_w.shape[1]
    alpha = (2 * depth) ** 0.25

    sm = jax.nn.softmax(hgrn_lb_logits.astype(F32), axis=0)
    lower_bounds = jnp.cumsum(sm, axis=0) - sm[0:1]

    xt = x
    sorted_init = None
    seq_rows = 256 * bn
    c_hg = (0, 1, 2, 3)
    c_su = 4
    c_rg = (5, 6)
    c_gate = 7
    n_exp = router_w.shape[2]
    buf_rows = _tile_buffer_rows(MOE_TILE, n_exp, TOP_K)
    per_blk = FFN_ROWS // SUB_ROWS
    max_sub = n * TOP_K // SUB_ROWS + (n // MOE_TILE) * n_exp + n_exp * (per_blk - 1)
    n_sorted = -(-max_sub // per_blk) * FFN_ROWS
    for l in range(depth):
        proj, xt = _in_proj(xt, w_in, l, 2048, w)
        ya = _hgrn2(proj, c_hg, lower_bounds[l], hgrn_norm_w[l], bn, seq_rows)
        mats = _s5_matrices(s5_a_re[l], s5_a_im[l], s5_b_re[l], s5_b_im[l],
                            s5_c_re[l], s5_c_im[l], s5_log_dt[l])
        yb = _s5(proj, c_su, mats, s5_d[l], s5_glu_w[l], s5_glu_b[l], bn, seq_rows)
        yc = _rglru(proj, c_rg, rg_conv_w[l], rg_conv_b[l], rg_wa[l], rg_ba[l], rg_wx[l], rg_bx[l],
                    jax.nn.softplus(-rg_lambda[l].astype(F32)), bn, seq_rows)
        x1 = _merge(xt, ya, yb, yc, proj, c_gate, w_branch, b_gate[l], w_out,
                    ln1_g[l], ln1_b[l], l, alpha, 512)
        pos, wsel, seg = _router(x1, router_w[l].T, router_bias[l], MOE_TILE)
        sub_e, sub_dst, blk_e, n_used = _dispatch_plan(seg[:, :, 0], buf_rows, n_sorted)
        if sorted_init is None:
            sorted_init = jnp.zeros((n_sorted + 2 * buf_rows, d), BF16)
        xs = _dispatch(x1, pos, sorted_init, sub_e, sub_dst, buf_rows)
        ys = _experts(xs, blk_e, n_used, exp_w1, exp_w3, exp_w2, l)
        xt = _combine(x1, pos, wsel, ys, sub_e, sub_dst, sh_w1, sh_w3, sh_w2, ln2_g[l], ln2_b[l],
                      l, alpha, buf_rows, out_batch=bn if l == depth - 1 else 0)
        sorted_init = ys
    return xt
```

```python
import functools
import math

import numpy as np
import jax
import jax.numpy as jnp
from jax import lax
from jax.experimental import pallas as pl
from jax.experimental.pallas import tpu as pltpu

F32 = jnp.float32
BF16 = jnp.bfloat16

HG_HEADS = 4
HG_HEAD_DIM = 128
HG_CHUNK = 16
S5_GROUP_CH = 16
S5_STATE = 64
RG_BLOCK_W = 128
RG_CONV = 4
RG_C = 8.0
N_EXPERTS = 64
N_EXPERT_GROUPS = 8
TOPK_GROUPS = 4
TOP_K = 8
ROUTED_SCALE = 2.5
LN_EPS = 1e-5
RMS_EPS = 1e-6
LANES = 128
SUBLANES = 8
VMEM_LIMIT = 56 * 1024 * 1024
SUB_ROWS = 16
MOE_TILE = 512
FFN_ROWS = 1024
MM_ROWS = 512
SEQ_STEPS = 256
PROJ_ROWS = 2048
MERGE_ROWS = 512


def _cparams(sem):
    return pltpu.CompilerParams(dimension_semantics=sem, vmem_limit_bytes=VMEM_LIMIT)


def _sigmoid(x):
    return 1.0 / (1.0 + jnp.exp(-x))


def _silu(x):
    return x * _sigmoid(x)


def _gelu_tanh(x):
    c = math.sqrt(2.0 / math.pi)
    return 0.5 * x * (1.0 + jnp.tanh(c * (x + 0.044715 * (x * x * x))))


def _layernorm(z, g, b):
    mu = jnp.mean(z, axis=-1, keepdims=True)
    zc = z - mu
    var = jnp.mean(zc * zc, axis=-1, keepdims=True)
    return zc * lax.rsqrt(var + LN_EPS) * g + b


def _dot(a, b):
    return jnp.dot(a, b, preferred_element_type=F32)


def _interleave_batches(x_ref, slab_ref):
    bn, t, c = x_ref.shape
    for b in range(bn):
        for sl in range(c // LANES):
            slab_ref[sl, pl.ds(b, t, stride=bn), :] = x_ref[b, :, sl * LANES:(sl + 1) * LANES]
    return jnp.concatenate([slab_ref[sl] for sl in range(c // LANES)], axis=1)


def _split_batches(y, o_ref, slab_ref):
    bn, t, c = o_ref.shape
    for sl in range(c // LANES):
        slab_ref[sl] = y[:, sl * LANES:(sl + 1) * LANES]
    for b in range(bn):
        for sl in range(c // LANES):
            o_ref[b, :, sl * LANES:(sl + 1) * LANES] = slab_ref[sl, pl.ds(b, t, stride=bn), :]


def _matmul_kernel(x_ref, w_ref, o_ref, xb_ref):
    @pl.when(pl.program_id(1) == 0)
    def _():
        xb_ref[...] = x_ref[...].astype(BF16)

    o_ref[...] = _dot(xb_ref[...], w_ref[0].astype(BF16)).astype(o_ref.dtype)


def _matmul_bsd_kernel(x_ref, w_ref, o_ref, xt_ref, xb_ref, slab_ref):
    @pl.when(pl.program_id(1) == 0)
    def _():
        xt = _interleave_batches(x_ref, slab_ref)
        xt_ref[...] = xt
        xb_ref[...] = xt.astype(BF16)

    o_ref[...] = _dot(xb_ref[...], w_ref[0].astype(BF16)).astype(o_ref.dtype)


def _in_proj(x, w, layer, tm, tn):
    k, n = w.shape[1], w.shape[2]
    w_spec = pl.BlockSpec((1, k, tn), lambda i, j: (layer, 0, j))
    sem = _cparams(("parallel", "arbitrary"))
    if x.ndim == 2:
        m = x.shape[0]
        return pl.pallas_call(
            _matmul_kernel,
            grid=(m // tm, n // tn),
            in_specs=[pl.BlockSpec((tm, k), lambda i, j: (i, 0)), w_spec],
            out_specs=pl.BlockSpec((tm, tn), lambda i, j: (i, j)),
            out_shape=jax.ShapeDtypeStruct((m, n), BF16),
            scratch_shapes=[pltpu.VMEM((tm, k), BF16)],
            compiler_params=sem,
            name="in_proj",
        )(x, w), x
    bn, s, _ = x.shape
    m = bn * s
    return pl.pallas_call(
        _matmul_bsd_kernel,
        grid=(m // tm, n // tn),
        in_specs=[pl.BlockSpec((bn, tm // bn, k), lambda i, j: (0, i, 0)), w_spec],
        out_specs=[pl.BlockSpec((tm, tn), lambda i, j: (i, j)),
                   pl.BlockSpec((tm, k), lambda i, j: (i, 0))],
        out_shape=[jax.ShapeDtypeStruct((m, n), BF16), jax.ShapeDtypeStruct((m, k), F32)],
        scratch_shapes=[pltpu.VMEM((tm, k), BF16), pltpu.VMEM((k // LANES, tm, LANES), F32)],
        compiler_params=sem,
        name="in_proj_bsd",
    )(x, w)


def _hgrn2_kernel(q_ref, f_ref, v_ref, g_ref, lb_ref, nw_ref, o_ref, *rest, batch, fill_per_step):
    if len(rest) == 5:
        z_hbm, st_ref, kvb_ref, zbuf_ref, zsem = rest
    else:
        (st_ref, kvb_ref), z_hbm = rest, None
    rows = q_ref.shape[0]
    cr = HG_CHUNK * batch
    n_chunks = rows // cr
    dh = HG_HEAD_DIM

    @pl.when(pl.program_id(0) == 0)
    def _():
        st_ref[...] = jnp.zeros_like(st_ref)
        kvb_ref[...] = jnp.zeros_like(kvb_ref)
        if z_hbm is not None:
            zbuf_ref[...] = jnp.zeros_like(zbuf_ref)

    fills = []
    if z_hbm is not None:
        zrows = zbuf_ref.shape[0]
        for kf in range(fill_per_step):
            z0 = pl.multiple_of((pl.program_id(0) * fill_per_step + kf) * zrows, zrows)
            fills.append(pltpu.make_async_copy(zbuf_ref, z_hbm.at[pl.ds(z0, zrows)], zsem))
        for cp in fills:
            cp.start()

    row = lax.broadcasted_iota(jnp.int32, (cr, dh), 0)
    row_b = row % batch
    ones_sum = jnp.ones((dh, dh), BF16)

    def chunk(c, carry):
        r0 = pl.multiple_of(c * cr, cr)
        for h in range(HG_HEADS):
            ls = slice(h * dh, (h + 1) * dh)
            lb = lb_ref[:, ls]
            f = lb + (1.0 - lb) * _sigmoid(f_ref[pl.ds(r0, cr), ls].astype(F32))
            q = _silu(q_ref[pl.ds(r0, cr), ls].astype(F32))
            k = 1.0 - f
            v = v_ref[pl.ds(r0, cr), ls].astype(F32)
            bc = jnp.log2(f)
            sh = batch
            while sh < cr:
                bc = bc + jnp.where(row >= sh, pltpu.roll(bc, sh, 0), 0.0)
                sh *= 2
            p0 = (q * k).astype(BF16)
            o = _dot(p0, ones_sum) * v
            for idx, a in enumerate((k, v, bc)):
                kvb_ref[h, idx, SUBLANES:SUBLANES + cr, :] = a
            for j in range(1, HG_CHUNK):
                s = j * batch
                lo = SUBLANES * (s // SUBLANES)
                n = cr - lo
                start = SUBLANES - (s - lo)
                ks, vs, bs = (kvb_ref[h, idx, start:start + n, :] for idx in range(3))
                p = q[lo:cr] * ks * jnp.exp2(bc[lo:cr] - bs)
                if s != lo:
                    p = jnp.where(row[0:n] >= batch, p, 0.0)
                upd = _dot(p.astype(BF16), ones_sum) * vs
                o = jnp.concatenate([o[0:lo], o[lo:cr] + upd], axis=0) if lo else o + upd
            b_last = bc[cr - batch:cr, :]
            qt = q * jnp.exp2(bc)
            kt = k * jnp.exp2(jnp.concatenate([b_last] * HG_CHUNK, axis=0) - bc)
            qm = jnp.concatenate([jnp.where(row_b == b, qt, 0.0) for b in range(batch)], axis=1)
            km = jnp.concatenate([jnp.where(row_b == b, kt, 0.0) for b in range(batch)], axis=1)
            st = st_ref[h]
            o = o + lax.dot_general(qm.astype(BF16), st.astype(BF16),
                                    (((1,), (1,)), ((), ())), preferred_element_type=F32)
            kv = lax.dot_general(v.astype(BF16), km.astype(BF16),
                                 (((0,), (0,)), ((), ())), preferred_element_type=F32)
            dec_all = jnp.concatenate([jnp.exp2(b_last[b:b + 1, :]) for b in range(batch)], axis=1)
            st_ref[h] = st * dec_all + kv
            o = o * lax.rsqrt(jnp.mean(o * o, axis=-1, keepdims=True) + RMS_EPS)
            o_ref[pl.ds(r0, cr), ls] = o * nw_ref[:, ls] * _silu(g_ref[pl.ds(r0, cr), ls].astype(F32))
        return carry

    lax.fori_loop(0, n_chunks, chunk, 0)
    for cp in fills:
        cp.wait()


ZERO_FILL_ROWS = 1024


def _hgrn2(proj, cols, lb, norm_w, batch, rows, zero_fill=None):
    assert 2 * batch == SUBLANES, "two time steps must fill the 8 sublanes"
    n = proj.shape[0]
    w = HG_HEADS * HG_HEAD_DIM
    cq, cf, cv, cg = cols

    def spec(cb):
        return pl.BlockSpec((rows, w), lambda i: (i, cb))

    vec = pl.BlockSpec((1, w), lambda i: (0, 0))
    out_specs = [pl.BlockSpec((rows, w), lambda i: (i, 0))]
    out_shape = [jax.ShapeDtypeStruct((n, w), F32)]
    scratch = [pltpu.VMEM((HG_HEADS, HG_HEAD_DIM, batch * HG_HEAD_DIM), F32),
               pltpu.VMEM((HG_HEADS, 3, SUBLANES + HG_CHUNK * batch, HG_HEAD_DIM), F32)]
    fill_per_step = 0
    if zero_fill:
        fill_per_step, rem = divmod(zero_fill[0], ZERO_FILL_ROWS * (n // rows))
        assert rem == 0, "the fill is split evenly over the grid steps"
        out_specs.append(pl.BlockSpec(memory_space=pl.ANY))
        out_shape.append(jax.ShapeDtypeStruct(zero_fill, BF16))
        scratch += [pltpu.VMEM((ZERO_FILL_ROWS, zero_fill[1]), BF16), pltpu.SemaphoreType.DMA(())]
    outs = pl.pallas_call(
        functools.partial(_hgrn2_kernel, batch=batch, fill_per_step=fill_per_step),
        grid=(n // rows,),
        in_specs=[spec(cq), spec(cf), spec(cv), spec(cg), vec, vec],
        out_specs=out_specs,
        out_shape=out_shape,
        scratch_shapes=scratch,
        compiler_params=_cparams(("arbitrary",)),
        name="hgrn2",
    )(proj, proj, proj, proj, lb.reshape(1, w), norm_w.reshape(1, w))
    return outs if zero_fill else outs[0]


S5_TILE = 128
S5_MM_ROWS = 512


def _s5_kernel(u_ref, pin_ref, bm_ref, ar_ref, ai_ref, cm_ref, d_ref, gw_ref, gb_ref,
               o_ref, u2_ref, hs_ref, h_ref, ys_ref, *, batch):
    rows, w = u_ref.shape
    nblk = w // LANES
    groups = hs_ref.shape[0]
    gpb = groups // nblk
    tile2 = 2 * S5_TILE
    pair = 2 * batch
    steps = rows // batch

    @pl.when(pl.program_id(0) == 0)
    def _():
        h_ref[...] = jnp.zeros_like(h_ref)

    for t in range(rows // S5_TILE):
        ub = u_ref[t * S5_TILE:(t + 1) * S5_TILE, :].astype(BF16)
        up = _dot(pin_ref[...], ub).astype(BF16)
        u2_ref[t * tile2:(t + 1) * tile2, 0:w] = up[0:tile2]
        u2_ref[t * tile2:(t + 1) * tile2, w:2 * w] = up[tile2:2 * tile2]
    mm = min(S5_MM_ROWS, 2 * rows)
    for j in range(nblk):
        for c in range(2 * rows // mm):
            rs = slice(c * mm, (c + 1) * mm)
            lhs = jnp.concatenate([u2_ref[rs, j * LANES:(j + 1) * LANES],
                                   u2_ref[rs, w + j * LANES:w + (j + 1) * LANES]], axis=1)
            bu = _dot(lhs, bm_ref[j])
            for q in range(gpb):
                hs_ref[j * gpb + q, rs, :] = bu[:, q * LANES:(q + 1) * LANES]

    top = lax.broadcasted_iota(jnp.int32, (pair, LANES), 0) < batch
    per_pass = 8
    unroll = 4
    for p in range(groups // per_pass):
        gs = [p * per_pass + g for g in range(per_pass)]
        ar8 = [jnp.broadcast_to(ar_ref[:, g * LANES:(g + 1) * LANES], (pair, LANES)) for g in gs]
        ai8 = [jnp.where(top, -1.0, 1.0) * ai_ref[:, g * LANES:(g + 1) * LANES] for g in gs]

        def body(i, hs):
            hs = list(hs)
            for s in range(unroll):
                r0 = pl.multiple_of((i * unroll + s) * pair, pair)
                for k, g in enumerate(gs):
                    h = ar8[k] * hs[k] + ai8[k] * pltpu.roll(hs[k], batch, 0) + hs_ref[g, pl.ds(r0, pair), :]
                    hs_ref[g, pl.ds(r0, pair), :] = h
                    hs[k] = h
            return tuple(hs)

        h0 = tuple(h_ref[:, g * LANES:(g + 1) * LANES] for g in gs)
        hn = lax.fori_loop(0, steps // unroll, body, h0)
        for k, g in enumerate(gs):
            h_ref[:, g * LANES:(g + 1) * LANES] = hn[k]

    for b in range(batch):
        for j in range(nblk):
            parts = [hs_ref[j * gpb + q, pl.ds(off + b, steps, stride=pair), :]
                     for off in (0, batch) for q in range(gpb)]
            lhs = jnp.concatenate(parts, axis=1).astype(BF16)
            ys_ref[j, pl.ds(b, steps, stride=batch), :] = _dot(lhs, cm_ref[j])
    y = jnp.concatenate([ys_ref[j] for j in range(nblk)], axis=1) + d_ref[...] * u_ref[...].astype(F32)
    y = _gelu_tanh(y)
    z = _dot(y.astype(BF16), gw_ref[...].astype(BF16)) + gb_ref[...]
    o_ref[...] = y * _sigmoid(z)


def _s5_layout_matrix(batch):
    assert 2 * batch == SUBLANES, "one time step must fill the 8 sublanes"
    tile2 = 2 * S5_TILE
    r2 = np.arange(tile2)
    src = (r2 // (2 * batch)) * batch + r2 % batch
    is_re = (r2 % (2 * batch)) < batch
    onehot = (src[:, None] == np.arange(S5_TILE)[None, :])
    pin = np.concatenate([onehot & is_re[:, None], onehot & ~is_re[:, None]], axis=0)
    return jnp.asarray(pin.astype(np.float32), BF16)


def _s5(proj, col, mats, d, glu_w, glu_b, batch, rows):
    n = proj.shape[0]
    w = d.shape[0]
    bmat, ar, ai, cmat = mats
    nblk, _, cpb = bmat.shape
    nc = nblk * cpb
    pin = _s5_layout_matrix(batch)
    full = lambda a: pl.BlockSpec(a.shape, lambda i, nd=a.ndim: (0,) * nd)
    return pl.pallas_call(
        functools.partial(_s5_kernel, batch=batch),
        grid=(n // rows,),
        in_specs=[pl.BlockSpec((rows, w), lambda i: (i, col)),
                  full(pin), full(bmat), full(ar), full(ai), full(cmat),
                  pl.BlockSpec((1, w), lambda i: (0, 0)),
                  pl.BlockSpec((w, w), lambda i: (0, 0)),
                  pl.BlockSpec((1, w), lambda i: (0, 0))],
        out_specs=pl.BlockSpec((rows, w), lambda i: (i, 0)),
        out_shape=jax.ShapeDtypeStruct((n, w), F32),
        scratch_shapes=[pltpu.VMEM((2 * rows, 2 * w), BF16),
                        pltpu.VMEM((nc // LANES, 2 * rows, LANES), F32),
                        pltpu.VMEM((2 * batch, nc), F32),
                        pltpu.VMEM((w // LANES, rows, LANES), F32)],
        compiler_params=_cparams(("arbitrary",)),
        name="s5",
    )(proj, pin, bmat, ar, ai, cmat, d.reshape(1, w), glu_w, glu_b.reshape(1, w))


def _s5_matrices(a_re, a_im, b_re, b_im, c_re, c_im, log_dt):
    g, p = a_re.shape
    ch = b_re.shape[-1]
    gpb = LANES // ch
    nblk = g // gpb
    dt = jnp.exp(log_dt.astype(F32))[:, None]
    mag = jnp.exp(a_re * dt)
    abr = mag * jnp.cos(a_im * dt)
    abi = mag * jnp.sin(a_im * dt)
    den = a_re * a_re + a_im * a_im
    cr = ((abr - 1.0) * a_re + abi * a_im) / den
    ci = (abi * a_re - (abr - 1.0) * a_im) / den
    bbr = cr[..., None] * b_re - ci[..., None] * b_im
    bbi = cr[..., None] * b_im + ci[..., None] * b_re
    eye = jnp.eye(gpb, dtype=F32)

    def in_block(m):
        m = m.reshape(nblk, gpb, p, ch)
        return jnp.einsum('jgpc,gh->jgchp', m, eye).reshape(nblk, gpb * ch, gpb * p)

    def out_block(m):
        m = m.reshape(nblk, gpb, ch, p)
        return jnp.einsum('jgcp,gh->jgphc', m, eye).reshape(nblk, gpb * p, gpb * ch)

    bmat = jnp.concatenate([in_block(bbr), in_block(bbi)], axis=1).astype(BF16)
    cmat = jnp.concatenate([out_block(c_re), -out_block(c_im)], axis=1).astype(BF16)
    return bmat, abr.reshape(1, g * p), abi.reshape(1, g * p), cmat


def _rglru_kernel(xg_ref, xr_ref, cw_ref, cb_ref, wa_ref, ba_ref, wx_ref, bx_ref, sp_ref,
                  o_ref, xp_ref, a_ref, b_ref, h_ref, *, batch):
    rows, w = xr_ref.shape
    halo = SUBLANES * ((RG_CONV - 1) * batch // SUBLANES + 1)
    pair = 2 * batch

    @pl.when(pl.program_id(0) == 0)
    def _():
        xp_ref[0:halo, :] = jnp.zeros((halo, w), F32)
        h_ref[...] = jnp.zeros_like(h_ref)

    xr = xr_ref[...].astype(F32)
    xp_ref[halo:halo + rows, :] = xr
    xc = cb_ref[...] + cw_ref[RG_CONV - 1:RG_CONV, :] * xr
    for i in range(RG_CONV - 1):
        back = (RG_CONV - 1 - i) * batch
        xc = xc + cw_ref[i:i + 1, :] * xp_ref[halo - back:halo - back + rows, :]
    xp_ref[0:halo, :] = xr[rows - halo:rows, :]

    xcb = xc.astype(BF16)
    nb = w // RG_BLOCK_W
    ra, ri = [], []
    for hblk in range(nb):
        xs = xcb[:, hblk * RG_BLOCK_W:(hblk + 1) * RG_BLOCK_W]
        ra.append(_dot(xs, wa_ref[hblk].astype(BF16)))
        ri.append(_dot(xs, wx_ref[hblk].astype(BF16)))
    r = _sigmoid(jnp.concatenate(ra, axis=1) + ba_ref[...])
    ig = _sigmoid(jnp.concatenate(ri, axis=1) + bx_ref[...])
    log_a = (-RG_C) * r * sp_ref[...]
    a = jnp.exp(log_a)
    a_ref[...] = a
    b_ref[...] = jnp.sqrt(-jnp.tanh(log_a) * (a * a + 1.0)) * (ig * xc)

    top = lax.broadcasted_iota(jnp.int32, (pair, w), 0) < batch
    unroll = 4

    def body(i, h):
        for s in range(unroll):
            r0 = pl.multiple_of((i * unroll + s) * pair, pair)
            at = a_ref[pl.ds(r0, pair), :]
            bt = b_ref[pl.ds(r0, pair), :]
            n1 = at * h + bt
            h1 = jnp.where(top, n1, pltpu.roll(n1, batch, 0))
            n2 = at * h1 + bt
            b_ref[pl.ds(r0, pair), :] = jnp.where(top, n1, n2)
            h = jnp.where(top, pltpu.roll(n2, batch, 0), n2)
        return h

    h_ref[...] = lax.fori_loop(0, rows // (pair * unroll), body, h_ref[...])
    o_ref[...] = b_ref[...] * _gelu_tanh(xg_ref[...].astype(F32))


def _rglru(proj, cols, conv_w, conv_b, wa, ba, wx, bx, softplus_neg_lam, batch, rows):
    assert 2 * batch == SUBLANES, "two time steps must fill the 8 sublanes"
    n = proj.shape[0]
    w = conv_b.shape[0]
    cg, cx = cols
    nb = wa.shape[0]
    halo = SUBLANES * ((RG_CONV - 1) * batch // SUBLANES + 1)
    vec = pl.BlockSpec((1, w), lambda i: (0, 0))
    blk = pl.BlockSpec((nb, RG_BLOCK_W, RG_BLOCK_W), lambda i: (0, 0, 0))
    return pl.pallas_call(
        functools.partial(_rglru_kernel, batch=batch),
        grid=(n // rows,),
        in_specs=[pl.BlockSpec((rows, w), lambda i: (i, cg)),
                  pl.BlockSpec((rows, w), lambda i: (i, cx)),
                  pl.BlockSpec((RG_CONV, w), lambda i: (0, 0)),
                  vec, blk, vec, blk, vec, vec],
        out_specs=pl.BlockSpec((rows, w), lambda i: (i, 0)),
        out_shape=jax.ShapeDtypeStruct((n, w), F32),
        scratch_shapes=[pltpu.VMEM((halo + rows, w), F32),
                        pltpu.VMEM((rows, w), F32),
                        pltpu.VMEM((rows, w), F32),
                        pltpu.VMEM((2 * batch, w), F32)],
        compiler_params=_cparams(("arbitrary",)),
        name="rglru",
    )(proj, proj, conv_w, conv_b.reshape(1, w), wa, ba.reshape(1, w), wx, bx.reshape(1, w),
      softplus_neg_lam.reshape(1, w))


def _merge_kernel(x_ref, ya_ref, yb_ref, yc_ref, *rest, alpha, halves):
    n_gate = 3 * halves
    gate_refs = rest[:n_gate]
    wb_ref, bg_ref, wo_ref, lg_ref, lb_ref, o_ref = rest[n_gate:]
    m = None
    for kbr, y_ref in enumerate((ya_ref, yb_ref, yc_ref)):
        br = _dot(y_ref[...].astype(BF16), wb_ref[0, kbr].astype(BF16))
        gp = jnp.concatenate([gate_refs[kbr * halves + i][...] for i in range(halves)], axis=1)
        t = _sigmoid(gp.astype(F32) + bg_ref[kbr:kbr + 1, :]) * br
        m = t if m is None else m + t
    mix = _dot(m.astype(BF16), wo_ref[0].astype(BF16))
    o_ref[...] = _layernorm(alpha * x_ref[...] + mix, lg_ref[...], lb_ref[...])


def _merge(x, ya, yb, yc, proj, gate_col0, w_branch, b_gate, w_out, ln_g, ln_b, layer, alpha, tm):
    n, d = x.shape
    w = ya.shape[1]
    nbr = w_branch.shape[1]
    halves = d // w
    row = lambda width: pl.BlockSpec((tm, width), lambda i: (i, 0))
    gates = [pl.BlockSpec((tm, w), lambda i, c=gate_col0 + c: (i, c)) for c in range(nbr * halves)]
    vec = pl.BlockSpec((1, d), lambda i: (0, 0))
    return pl.pallas_call(
        functools.partial(_merge_kernel, alpha=alpha, halves=halves),
        grid=(n // tm,),
        in_specs=[row(d), row(w), row(w), row(w)] + gates + [
            pl.BlockSpec((1, nbr, w, d), lambda i: (layer, 0, 0, 0)),
            pl.BlockSpec((nbr, d), lambda i: (0, 0)),
            pl.BlockSpec((1, d, d), lambda i: (layer, 0, 0)), vec, vec],
        out_specs=row(d),
        out_shape=jax.ShapeDtypeStruct((n, d), F32),
        compiler_params=_cparams(("parallel",)),
        name="merge",
    )(x, ya, yb, yc, *([proj] * (nbr * halves)), w_branch, b_gate.reshape(nbr, d), w_out,
      ln_g.reshape(1, d), ln_b.reshape(1, d))


def _first_max_mask(cur, idx, axis):
    m = jnp.max(cur, axis=axis, keepdims=True)
    first = jnp.min(jnp.where(cur == m, idx, jnp.int32(2 ** 30)), axis=axis, keepdims=True)
    return idx == first


def _router_kernel(x_ref, wr_ref, rb_ref, pos_ref, w_ref, seg_ref):
    tm = x_ref.shape[0]
    e = N_EXPERTS
    per = e // N_EXPERT_GROUPS
    x = x_ref[...]
    wr = wr_ref[...]
    xh = x.astype(BF16)
    xl = (x - xh.astype(F32)).astype(BF16)
    wh = wr.astype(BF16)
    wl = (wr - wh.astype(F32)).astype(BF16)
    nt = (((1,), (1,)), ((), ()))
    logits = (lax.dot_general(wh, xh, nt, preferred_element_type=F32)
              + lax.dot_general(wh, xl, nt, preferred_element_type=F32)
              + lax.dot_general(wl, xh, nt, preferred_element_type=F32))
    scores = _sigmoid(logits)
    biased = scores + rb_ref[...]
    neg = jnp.float32(-jnp.inf)

    b3 = biased.reshape(N_EXPERT_GROUPS, per, tm)
    i3 = lax.broadcasted_iota(jnp.int32, b3.shape, 1)
    top1 = _first_max_mask(b3, i3, 1)
    m1 = jnp.max(b3, axis=1, keepdims=True)
    m2 = jnp.max(jnp.where(top1, neg, b3), axis=1, keepdims=True)
    gscore = (m1 + m2).reshape(N_EXPERT_GROUPS, tm)

    ig = lax.broadcasted_iota(jnp.int32, gscore.shape, 0)
    gsel = jnp.zeros(gscore.shape, F32)
    cur = gscore
    for _ in range(TOPK_GROUPS):
        pick = _first_max_mask(cur, ig, 0)
        gsel = jnp.where(pick, 1.0, gsel)
        cur = jnp.where(pick, neg, cur)

    gsel3 = jnp.broadcast_to(gsel.reshape(N_EXPERT_GROUPS, 1, tm), b3.shape)
    masked = jnp.where(gsel3 > 0.0, b3, neg).reshape(e, tm)
    ie = lax.broadcasted_iota(jnp.int32, masked.shape, 0)
    chosen = jnp.zeros(masked.shape, F32)
    cur = masked
    for _ in range(TOP_K):
        pick = _first_max_mask(cur, ie, 0)
        chosen = jnp.where(pick, 1.0, chosen)
        cur = jnp.where(pick, neg, cur)

    wsel = jnp.where(chosen > 0.0, scores, 0.0)
    wsel = wsel / jnp.sum(wsel, axis=0, keepdims=True) * ROUTED_SCALE
    w_ref[0] = wsel

    cb = chosen.astype(BF16)
    tok_r = lax.broadcasted_iota(jnp.int32, (tm, tm), 0)
    tok_c = lax.broadcasted_iota(jnp.int32, (tm, tm), 1)
    before = jnp.where(tok_r < tok_c, 1.0, 0.0).astype(BF16)
    prefix = _dot(cb, before)
    cnt = jnp.sum(chosen, axis=1, keepdims=True)
    seg = jnp.floor((cnt + (SUB_ROWS - 1)) * (1.0 / SUB_ROWS))
    seg_b = jnp.broadcast_to(seg, (e, LANES))
    ex_r = lax.broadcasted_iota(jnp.int32, (e, e), 0)
    ex_c = lax.broadcasted_iota(jnp.int32, (e, e), 1)
    earlier = jnp.where(ex_c < ex_r, 1.0, 0.0).astype(BF16)
    seg_off = _dot(earlier, seg_b.astype(BF16))
    seg_ref[0] = seg_b
    pos_ref[0] = jnp.where(chosen > 0.0, seg_off[:, 0:1] * SUB_ROWS + prefix, -1.0)


def _router(x, router_w_t, router_bias, tm):
    n, d = x.shape
    e = router_w_t.shape[0]
    nt = n // tm
    tile = lambda width: pl.BlockSpec((1, e, width), lambda i: (i, 0, 0))
    return pl.pallas_call(
        _router_kernel,
        grid=(nt,),
        in_specs=[pl.BlockSpec((tm, d), lambda i: (i, 0)),
                  pl.BlockSpec((e, d), lambda i: (0, 0)),
                  pl.BlockSpec((e, 1), lambda i: (0, 0))],
        out_specs=[tile(tm), tile(tm), tile(LANES)],
        out_shape=[jax.ShapeDtypeStruct((nt, e, tm), F32),
                   jax.ShapeDtypeStruct((nt, e, tm), F32),
                   jax.ShapeDtypeStruct((nt, e, LANES), F32)],
        compiler_params=_cparams(("parallel",)),
        name="router",
    )(x, router_w_t, router_bias.reshape(e, 1))


def _tile_buffer_rows(tile, n_experts, top_k):
    rows = tile * top_k + n_experts * (SUB_ROWS - 1)
    return -(-rows // MM_ROWS) * MM_ROWS


def _slot_matrix(j, e, pos_ref, val_ref, tile):
    e = jnp.maximum(e, 0)
    prow = pos_ref[0, pl.ds(e, 1), :] - jnp.asarray(j * SUB_ROWS).astype(F32)
    tgt = lax.broadcasted_iota(jnp.int32, (SUB_ROWS, tile), 0).astype(F32)
    if val_ref is None:
        return jnp.where(prow == tgt, 1.0, 0.0).astype(BF16)
    return jnp.where(prow == tgt, val_ref[0, pl.ds(e, 1), :], 0.0).astype(BF16)


LOOP_UNROLL = 8


def _wait_sub_blocks(src, dst, sem, n_sub):
    def wait(i, c):
        for _ in range(LOOP_UNROLL):
            pltpu.make_async_copy(src.at[pl.ds(0, SUB_ROWS)], dst.at[pl.ds(0, SUB_ROWS)], sem).wait()
        return c

    lax.fori_loop(0, n_sub // LOOP_UNROLL, wait, 0)


def _dispatch_kernel(sub_e_ref, sub_dst_ref, x_ref, pos_ref, init_ref, xs_hbm, m_ref, xs_ref, sem):
    del init_ref
    t = pl.program_id(0)
    tile = x_ref.shape[0]
    n_sub = m_ref.shape[0] // SUB_ROWS
    base = t * n_sub
    slot = t % 2

    def build(j, c):
        r0 = pl.multiple_of(j * SUB_ROWS, SUB_ROWS)
        m_ref[pl.ds(r0, SUB_ROWS), :] = _slot_matrix(j, sub_e_ref[base + j], pos_ref, None, tile)
        return c

    lax.fori_loop(0, n_sub, build, 0, unroll=LOOP_UNROLL)
    xb = x_ref[...].astype(BF16)
    for c in range(m_ref.shape[0] // MM_ROWS):
        rs = slice(c * MM_ROWS, (c + 1) * MM_ROWS)
        xs_ref[slot, rs, :] = _dot(m_ref[rs, :], xb).astype(BF16)

    def start(j, c):
        r0 = pl.multiple_of(j * SUB_ROWS, SUB_ROWS)
        d0 = pl.multiple_of(sub_dst_ref[base + j] * SUB_ROWS, SUB_ROWS)
        pltpu.make_async_copy(xs_ref.at[slot, pl.ds(r0, SUB_ROWS)], xs_hbm.at[pl.ds(d0, SUB_ROWS)],
                              sem.at[slot]).start()
        return c

    lax.fori_loop(0, n_sub, start, 0, unroll=LOOP_UNROLL)

    @pl.when(t > 0)
    def _():
        _wait_sub_blocks(xs_ref.at[1 - slot], xs_hbm, sem.at[1 - slot], n_sub)

    @pl.when(t == pl.num_programs(0) - 1)
    def _():
        _wait_sub_blocks(xs_ref.at[slot], xs_hbm, sem.at[slot], n_sub)


def _dispatch(x, pos, init, sub_e, sub_dst, buf_rows):
    nt, e, tile = pos.shape
    d = init.shape[1]
    return pl.pallas_call(
        _dispatch_kernel,
        grid_spec=pltpu.PrefetchScalarGridSpec(
            num_scalar_prefetch=2,
            grid=(nt,),
            in_specs=[pl.BlockSpec((tile, d), lambda i, *_: (i, 0)),
                      pl.BlockSpec((1, e, tile), lambda i, *_: (i, 0, 0)),
                      pl.BlockSpec(memory_space=pl.ANY)],
            out_specs=pl.BlockSpec(memory_space=pl.ANY),
            scratch_shapes=[pltpu.VMEM((buf_rows, tile), BF16),
                            pltpu.VMEM((2, buf_rows, d), BF16),
                            pltpu.SemaphoreType.DMA((2,))]),
        out_shape=jax.ShapeDtypeStruct(init.shape, BF16),
        input_output_aliases={4: 0},
        compiler_params=_cparams(("arbitrary",)),
        name="moe_dispatch",
    )(sub_e.reshape(-1), sub_dst.reshape(-1), x, pos, init)


def _experts_kernel(blk_e_ref, n_used_ref, xs_ref, w1_ref, w3_ref, w2_ref, ys_ref, w13_ref, w2b_ref):
    i = pl.program_id(0)
    hid = w1_ref.shape[-1]
    used = i < n_used_ref[0]
    new_expert = jnp.logical_or(i == 0, blk_e_ref[i] != blk_e_ref[jnp.maximum(i - 1, 0)])

    @pl.when(jnp.logical_and(used, new_expert))
    def _():
        w13_ref[:, 0:hid] = w1_ref[0, 0].astype(BF16)
        w13_ref[:, hid:2 * hid] = w3_ref[0, 0].astype(BF16)
        w2b_ref[...] = w2_ref[0, 0].astype(BF16)

    @pl.when(used)
    def _():
        h = _dot(xs_ref[...], w13_ref[...])
        hh = (_silu(h[:, 0:hid]) * h[:, hid:2 * hid]).astype(BF16)
        ys_ref[...] = _dot(hh, w2b_ref[...]).astype(ys_ref.dtype)


def _experts(xs, blk_e, n_used, w1, w3, w2, layer):
    rows, d = xs.shape
    hid = w1.shape[-1]
    nblk = rows // FFN_ROWS
    rowblk = pl.BlockSpec((FFN_ROWS, d), lambda i, be, nu: (jnp.minimum(i, nu[0] - 1), 0))
    return pl.pallas_call(
        _experts_kernel,
        grid_spec=pltpu.PrefetchScalarGridSpec(
            num_scalar_prefetch=2,
            grid=(nblk,),
            in_specs=[rowblk,
                      pl.BlockSpec((1, 1, d, hid), lambda i, be, nu: (layer, be[i], 0, 0)),
                      pl.BlockSpec((1, 1, d, hid), lambda i, be, nu: (layer, be[i], 0, 0)),
                      pl.BlockSpec((1, 1, hid, d), lambda i, be, nu: (layer, be[i], 0, 0))],
            out_specs=rowblk,
            scratch_shapes=[pltpu.VMEM((d, 2 * hid), BF16), pltpu.VMEM((hid, d), BF16)]),
        out_shape=jax.ShapeDtypeStruct((rows, d), BF16),
        input_output_aliases={2: 0},
        compiler_params=_cparams(("arbitrary",)),
        name="moe_experts",
    )(blk_e, n_used, xs, w1, w3, w2)


def _combine_kernel(sub_e_ref, sub_dst_ref, x_ref, pos_ref, w_ref, ys_hbm, s1_ref, s3_ref, s2_ref,
                    lg_ref, lb_ref, o_ref, m_ref, ys_ref, sem, *maybe_slab, alpha):
    t = pl.program_id(0)
    tile = x_ref.shape[0]
    n_sub = m_ref.shape[0] // SUB_ROWS
    base = t * n_sub
    slot = t % 2
    last = t == pl.num_programs(0) - 1

    def start(j, tile_base, sl):
        r0 = pl.multiple_of(j * SUB_ROWS, SUB_ROWS)
        d0 = pl.multiple_of(sub_dst_ref[tile_base + j] * SUB_ROWS, SUB_ROWS)
        pltpu.make_async_copy(ys_hbm.at[pl.ds(d0, SUB_ROWS)], ys_ref.at[sl, pl.ds(r0, SUB_ROWS)],
                              sem.at[sl]).start()

    def build(j, c):
        r0 = pl.multiple_of(j * SUB_ROWS, SUB_ROWS)
        m_ref[pl.ds(r0, SUB_ROWS), :] = _slot_matrix(j, sub_e_ref[base + j], pos_ref, w_ref, tile)
        return c

    def start_own(j, c):
        start(j, base, slot)
        return c

    def start_next(j, c):
        start(j, base + n_sub, 1 - slot)
        return c

    @pl.when(t == 0)
    def _():
        lax.fori_loop(0, n_sub, start_own, 0, unroll=LOOP_UNROLL)

    @pl.when(jnp.logical_not(last))
    def _():
        lax.fori_loop(0, n_sub, start_next, 0, unroll=LOOP_UNROLL)

    lax.fori_loop(0, n_sub, build, 0, unroll=LOOP_UNROLL)
    _wait_sub_blocks(ys_hbm, ys_ref.at[slot], sem.at[slot], n_sub)

    x = x_ref[...]
    xb = x.astype(BF16)
    g1 = _dot(xb, s1_ref[0].astype(BF16))
    g3 = _dot(xb, s3_ref[0].astype(BF16))
    ffn = _dot((_silu(g1) * g3).astype(BF16), s2_ref[0].astype(BF16))
    tn = (((0,), (0,)), ((), ()))
    for c in range(m_ref.shape[0] // MM_ROWS):
        rs = slice(c * MM_ROWS, (c + 1) * MM_ROWS)
        ffn = ffn + lax.dot_general(m_ref[rs, :], ys_ref[slot, rs, :], tn, preferred_element_type=F32)
    y = _layernorm(alpha * x + ffn, lg_ref[...], lb_ref[...])
    if maybe_slab:
        _split_batches(y, o_ref, maybe_slab[0])
    else:
        o_ref[...] = y


def _combine(x, pos, wsel, ys, sub_e, sub_dst, s1, s3, s2, ln_g, ln_b, layer, alpha, buf_rows, out_batch):
    n, d = x.shape
    nt, e, tile = pos.shape
    hid = s1.shape[-1]
    vec = pl.BlockSpec((1, d), lambda i, *_: (0, 0))
    scratch = [pltpu.VMEM((buf_rows, tile), BF16), pltpu.VMEM((2, buf_rows, d), BF16),
               pltpu.SemaphoreType.DMA((2,))]
    if out_batch:
        out = jax.ShapeDtypeStruct((out_batch, n // out_batch, d), F32)
        out_spec = pl.BlockSpec((out_batch, tile // out_batch, d), lambda i, *_: (0, i, 0))
        scratch.append(pltpu.VMEM((d // LANES, tile, LANES), F32))
    else:
        out = jax.ShapeDtypeStruct((n, d), F32)
        out_spec = pl.BlockSpec((tile, d), lambda i, *_: (i, 0))
    return pl.pallas_call(
        functools.partial(_combine_kernel, alpha=alpha),
        grid_spec=pltpu.PrefetchScalarGridSpec(
            num_scalar_prefetch=2,
            grid=(nt,),
            in_specs=[pl.BlockSpec((tile, d), lambda i, *_: (i, 0)),
                      pl.BlockSpec((1, e, tile), lambda i, *_: (i, 0, 0)),
                      pl.BlockSpec((1, e, tile), lambda i, *_: (i, 0, 0)),
                      pl.BlockSpec(memory_space=pl.ANY),
                      pl.BlockSpec((1, d, hid), lambda i, *_: (layer, 0, 0)),
                      pl.BlockSpec((1, d, hid), lambda i, *_: (layer, 0, 0)),
                      pl.BlockSpec((1, hid, d), lambda i, *_: (layer, 0, 0)), vec, vec],
            out_specs=out_spec,
            scratch_shapes=scratch),
        out_shape=out,
        compiler_params=_cparams(("arbitrary",)),
        name="moe_combine",
    )(sub_e.reshape(-1), sub_dst.reshape(-1), x, pos, wsel, ys, s1, s3, s2,
      ln_g.reshape(1, d), ln_b.reshape(1, d))


def _dispatch_plan(seg, buf_rows, n_sorted_rows, total_rows):
    nt, ne = seg.shape
    n_sub = buf_rows // SUB_ROWS
    per_blk = FFN_ROWS // SUB_ROWS
    seg = seg.astype(jnp.int32)
    seg_end = jnp.cumsum(seg, axis=1)
    seg_start = seg_end - seg
    exp_sub = jnp.sum(seg, axis=0)
    exp_blk = (exp_sub + per_blk - 1) // per_blk
    blk_end = jnp.cumsum(exp_blk)
    exp_start = (blk_end - exp_blk) * per_blk
    dst_start = exp_start[None, :] + jnp.cumsum(seg, axis=0) - seg
    j = jnp.arange(n_sub, dtype=jnp.int32)
    sub_e = jnp.sum((seg_end[:, None, :] <= j[None, :, None]).astype(jnp.int32), axis=-1)
    used = j[None, :] < seg_end[:, -1:]
    sub_ec = jnp.minimum(sub_e, ne - 1)
    pick = sub_ec[:, :, None] == jnp.arange(ne, dtype=jnp.int32)[None, None, :]
    dst = j[None, :] + jnp.sum(jnp.where(pick, (dst_start - seg_start)[:, None, :], 0), axis=-1)
    spare = n_sorted_rows // SUB_ROWS + (jnp.arange(nt, dtype=jnp.int32)[:, None] % 2) * n_sub + j[None, :]
    sub_dst = jnp.where(used, dst, spare).astype(jnp.int32)
    sub_e = jnp.where(used, sub_ec, -1).astype(jnp.int32)
    nblk = total_rows // FFN_ROWS
    i = jnp.arange(nblk, dtype=jnp.int32)
    blk_e = jnp.minimum(jnp.sum((blk_end[None, :] <= i[:, None]).astype(jnp.int32), axis=-1), ne - 1)
    return sub_e, sub_dst, blk_e, blk_end[-1:].astype(jnp.int32)


def kernel(x, w_in, b_gate, hgrn_lb_logits, hgrn_norm_w, s5_a_re, s5_a_im, s5_b_re, s5_b_im,
           s5_c_re, s5_c_im, s5_d, s5_log_dt, s5_glu_w, s5_glu_b, rg_conv_w, rg_conv_b,
           rg_wa, rg_ba, rg_wx, rg_bx, rg_lambda, w_branch, w_out, ln1_g, ln1_b,
           router_w, router_bias, exp_w1, exp_w3, exp_w2, sh_w1, sh_w3, sh_w2, ln2_g, ln2_b):
    bn, s, d = x.shape
    depth = w_in.shape[0]
    n = bn * s
    w = hgrn_norm_w.shape[1]
    alpha = (2 * depth) ** 0.25

    sm = jax.nn.softmax(hgrn_lb_logits.astype(F32), axis=0)
    lower_bounds = jnp.cumsum(sm, axis=0) - sm[0:1]

    xt = x
    sorted_init = None
    seq_rows = SEQ_STEPS * bn
    c_hg = (0, 1, 2, 3)
    c_su = 4
    c_rg = (5, 6)
    c_gate = 7
    n_exp = router_w.shape[2]
    buf_rows = _tile_buffer_rows(MOE_TILE, n_exp, TOP_K)
    per_blk = FFN_ROWS // SUB_ROWS
    max_sub = n * TOP_K // SUB_ROWS + (n // MOE_TILE) * n_exp + n_exp * (per_blk - 1)
    n_sorted = -(-max_sub // per_blk) * FFN_ROWS
    fill_unit = ZERO_FILL_ROWS * (n // seq_rows)
    sorted_rows = -(-(n_sorted + 2 * buf_rows) // fill_unit) * fill_unit
    for l in range(depth):
        proj, xt = _in_proj(xt, w_in, l, PROJ_ROWS, w)
        if sorted_init is None:
            ya, sorted_init = _hgrn2(proj, c_hg, lower_bounds[l], hgrn_norm_w[l], bn, seq_rows,
                                     zero_fill=(sorted_rows, d))
        else:
            ya = _hgrn2(proj, c_hg, lower_bounds[l], hgrn_norm_w[l], bn, seq_rows)
        mats = _s5_matrices(s5_a_re[l], s5_a_im[l], s5_b_re[l], s5_b_im[l],
                            s5_c_re[l], s5_c_im[l], s5_log_dt[l])
        yb = _s5(proj, c_su, mats, s5_d[l], s5_glu_w[l], s5_glu_b[l], bn, seq_rows)
        yc = _rglru(proj, c_rg, rg_conv_w[l], rg_conv_b[l], rg_wa[l], rg_ba[l], rg_wx[l], rg_bx[l],
                    jax.nn.softplus(-rg_lambda[l].astype(F32)), bn, seq_rows)
        x1 = _merge(xt, ya, yb, yc, proj, c_gate, w_branch, b_gate[l], w_out,
                    ln1_g[l], ln1_b[l], l, alpha, MERGE_ROWS)
        pos, wsel, seg = _router(x1, router_w[l].T, router_bias[l], MOE_TILE)
        sub_e, sub_dst, blk_e, n_used = _dispatch_plan(seg[:, :, 0], buf_rows, n_sorted, sorted_rows)
        xs = _dispatch(x1, pos, sorted_init, sub_e, sub_dst, buf_rows)
        ys = _experts(xs, blk_e, n_used, exp_w1, exp_w3, exp_w2, l)
        xt = _combine(x1, pos, wsel, ys, sub_e, sub_dst, sh_w1, sh_w3, sh_w2, ln2_g[l], ln2_b[l],
                      l, alpha, buf_rows, out_batch=bn if l == depth - 1 else 0)
        sorted_init = ys
    return xt
```

```python
import functools
import math

import numpy as np
import jax
import jax.numpy as jnp
from jax import lax
from jax.experimental import pallas as pl
from jax.experimental.pallas import tpu as pltpu

F32 = jnp.float32
BF16 = jnp.bfloat16

HG_HEADS = 4
HG_HEAD_DIM = 128
HG_CHUNK = 16
HG_PAIR = 2
S5_GROUP_CH = 16
S5_STATE = 64
RG_BLOCK_W = 128
RG_CONV = 4
RG_C = 8.0
N_EXPERTS = 64
N_EXPERT_GROUPS = 8
TOPK_GROUPS = 4
TOP_K = 8
ROUTED_SCALE = 2.5
LN_EPS = 1e-5
RMS_EPS = 1e-6
LANES = 128
SUBLANES = 8
VMEM_LIMIT = 56 * 1024 * 1024
SUB_ROWS = 16
MOE_TILE = 512
FFN_ROWS = 1024
MM_ROWS = 512
SEQ_STEPS = 256
PROJ_ROWS = 2048
MERGE_ROWS = 512


def _cparams(sem):
    return pltpu.CompilerParams(dimension_semantics=sem, vmem_limit_bytes=VMEM_LIMIT)


def _sigmoid(x):
    return 1.0 / (1.0 + jnp.exp(-x))


def _silu(x):
    return x * _sigmoid(x)


def _gelu_tanh(x):
    c = math.sqrt(2.0 / math.pi)
    return 0.5 * x * (1.0 + jnp.tanh(c * (x + 0.044715 * (x * x * x))))


def _layernorm(z, g, b):
    mu = jnp.mean(z, axis=-1, keepdims=True)
    zc = z - mu
    var = jnp.mean(zc * zc, axis=-1, keepdims=True)
    return zc * lax.rsqrt(var + LN_EPS) * g + b


def _dot(a, b):
    return jnp.dot(a, b, preferred_element_type=F32)


def _interleave_batches(x_ref, slab_ref):
    bn, t, c = x_ref.shape
    for b in range(bn):
        for sl in range(c // LANES):
            slab_ref[sl, pl.ds(b, t, stride=bn), :] = x_ref[b, :, sl * LANES:(sl + 1) * LANES]
    return jnp.concatenate([slab_ref[sl] for sl in range(c // LANES)], axis=1)


def _split_batches(y, o_ref, slab_ref):
    bn, t, c = o_ref.shape
    for sl in range(c // LANES):
        slab_ref[sl] = y[:, sl * LANES:(sl + 1) * LANES]
    for b in range(bn):
        for sl in range(c // LANES):
            o_ref[b, :, sl * LANES:(sl + 1) * LANES] = slab_ref[sl, pl.ds(b, t, stride=bn), :]


def _matmul_kernel(x_ref, w_ref, o_ref, xb_ref):
    @pl.when(pl.program_id(1) == 0)
    def _():
        xb_ref[...] = x_ref[...].astype(BF16)

    o_ref[...] = _dot(xb_ref[...], w_ref[0].astype(BF16)).astype(o_ref.dtype)


def _matmul_bsd_kernel(x_ref, w_ref, o_ref, xt_ref, xb_ref, slab_ref):
    @pl.when(pl.program_id(1) == 0)
    def _():
        xt = _interleave_batches(x_ref, slab_ref)
        xt_ref[...] = xt
        xb_ref[...] = xt.astype(BF16)

    o_ref[...] = _dot(xb_ref[...], w_ref[0].astype(BF16)).astype(o_ref.dtype)


def _in_proj(x, w, layer, tm, tn):
    k, n = w.shape[1], w.shape[2]
    w_spec = pl.BlockSpec((1, k, tn), lambda i, j: (layer, 0, j))
    sem = _cparams(("parallel", "arbitrary"))
    if x.ndim == 2:
        m = x.shape[0]
        return pl.pallas_call(
            _matmul_kernel,
            grid=(m // tm, n // tn),
            in_specs=[pl.BlockSpec((tm, k), lambda i, j: (i, 0)), w_spec],
            out_specs=pl.BlockSpec((tm, tn), lambda i, j: (i, j)),
            out_shape=jax.ShapeDtypeStruct((m, n), BF16),
            scratch_shapes=[pltpu.VMEM((tm, k), BF16)],
            compiler_params=sem,
            name="in_proj",
        )(x, w), x
    bn, s, _ = x.shape
    m = bn * s
    return pl.pallas_call(
        _matmul_bsd_kernel,
        grid=(m // tm, n // tn),
        in_specs=[pl.BlockSpec((bn, tm // bn, k), lambda i, j: (0, i, 0)), w_spec],
        out_specs=[pl.BlockSpec((tm, tn), lambda i, j: (i, j)),
                   pl.BlockSpec((tm, k), lambda i, j: (i, 0))],
        out_shape=[jax.ShapeDtypeStruct((m, n), BF16), jax.ShapeDtypeStruct((m, k), F32)],
        scratch_shapes=[pltpu.VMEM((tm, k), BF16), pltpu.VMEM((k // LANES, tm, LANES), F32)],
        compiler_params=sem,
        name="in_proj_bsd",
    )(x, w)


def _hgrn2_kernel(q_ref, f_ref, v_ref, g_ref, lb_ref, nw_ref, o_ref, *rest, batch, fill_per_step):
    if len(rest) == 5:
        z_hbm, st_ref, kvb_ref, zbuf_ref, zsem = rest
    else:
        (st_ref, kvb_ref), z_hbm = rest, None
    rows = q_ref.shape[0]
    cr = HG_CHUNK * batch
    n_chunks = rows // cr
    dh = HG_HEAD_DIM

    @pl.when(pl.program_id(0) == 0)
    def _():
        st_ref[...] = jnp.zeros_like(st_ref)
        kvb_ref[...] = jnp.zeros_like(kvb_ref)
        if z_hbm is not None:
            zbuf_ref[...] = jnp.zeros_like(zbuf_ref)

    fills = []
    if z_hbm is not None:
        zrows = zbuf_ref.shape[0]
        for kf in range(fill_per_step):
            z0 = pl.multiple_of((pl.program_id(0) * fill_per_step + kf) * zrows, zrows)
            fills.append(pltpu.make_async_copy(zbuf_ref, z_hbm.at[pl.ds(z0, zrows)], zsem))
        for cp in fills:
            cp.start()

    tr = HG_PAIR * cr
    row = lax.broadcasted_iota(jnp.int32, (tr, dh), 0)
    row_c = row % cr
    row_b = row[0:cr] % batch
    ones_sum = jnp.ones((dh, dh), BF16)

    def chunks(c, carry):
        r0 = pl.multiple_of(c * tr, tr)
        for h in range(HG_HEADS):
            ls = slice(h * dh, (h + 1) * dh)
            lb = lb_ref[:, ls]
            f = lb + (1.0 - lb) * _sigmoid(f_ref[pl.ds(r0, tr), ls].astype(F32))
            q = _silu(q_ref[pl.ds(r0, tr), ls].astype(F32))
            k = 1.0 - f
            v = v_ref[pl.ds(r0, tr), ls].astype(F32)
            bc = jnp.log2(f)
            sh = batch
            while sh < cr:
                bc = bc + jnp.where(row_c >= sh, pltpu.roll(bc, sh, 0), 0.0)
                sh *= 2
            p0 = (q * k).astype(BF16)
            o = _dot(p0, ones_sum) * v
            for idx, a in enumerate((k, v, bc)):
                kvb_ref[h, idx, SUBLANES:SUBLANES + tr, :] = a
            for j in range(1, HG_CHUNK):
                s = j * batch
                lo = SUBLANES * (s // SUBLANES)
                n = cr - lo
                start = SUBLANES - (s - lo)
                own = lambda a: jnp.concatenate([a[ci * cr + lo:(ci + 1) * cr] for ci in range(HG_PAIR)], axis=0)
                ks, vs, bs = (jnp.concatenate(
                    [kvb_ref[h, idx, ci * cr + start:ci * cr + start + n, :] for ci in range(HG_PAIR)], axis=0)
                    for idx in range(3))
                p = own(q) * ks * jnp.exp2(own(bc) - bs)
                if s != lo:
                    p = jnp.where(own(row_c) >= s, p, 0.0)
                upd = _dot(p.astype(BF16), ones_sum) * vs
                pieces = []
                for ci in range(HG_PAIR):
                    pieces += [o[ci * cr:ci * cr + lo], o[ci * cr + lo:(ci + 1) * cr] + upd[ci * n:(ci + 1) * n]]
                o = jnp.concatenate([pc for pc in pieces if pc.shape[0]], axis=0)
            st = st_ref[h]
            for ci in range(HG_PAIR):
                rs = slice(ci * cr, (ci + 1) * cr)
                bcc = bc[rs]
                b_last = bcc[cr - batch:cr, :]
                qt = q[rs] * jnp.exp2(bcc)
                kt = k[rs] * jnp.exp2(jnp.concatenate([b_last] * HG_CHUNK, axis=0) - bcc)
                qm = jnp.concatenate([jnp.where(row_b == b, qt, 0.0) for b in range(batch)], axis=1)
                km = jnp.concatenate([jnp.where(row_b == b, kt, 0.0) for b in range(batch)], axis=1)
                oc = o[rs] + lax.dot_general(qm.astype(BF16), st.astype(BF16),
                                             (((1,), (1,)), ((), ())), preferred_element_type=F32)
                kv = lax.dot_general(v[rs].astype(BF16), km.astype(BF16),
                                     (((0,), (0,)), ((), ())), preferred_element_type=F32)
                dec_all = jnp.concatenate([jnp.exp2(b_last[b:b + 1, :]) for b in range(batch)], axis=1)
                st = st * dec_all + kv
                oc = oc * lax.rsqrt(jnp.mean(oc * oc, axis=-1, keepdims=True) + RMS_EPS)
                rc = pl.multiple_of(r0 + ci * cr, cr)
                o_ref[pl.ds(rc, cr), ls] = oc * nw_ref[:, ls] * _silu(g_ref[pl.ds(rc, cr), ls].astype(F32))
            st_ref[h] = st
        return carry

    lax.fori_loop(0, n_chunks // HG_PAIR, chunks, 0)
    for cp in fills:
        cp.wait()


ZERO_FILL_ROWS = 1024


def _hgrn2(proj, cols, lb, norm_w, batch, rows, zero_fill=None):
    assert 2 * batch == SUBLANES, "two time steps must fill the 8 sublanes"
    n = proj.shape[0]
    w = HG_HEADS * HG_HEAD_DIM
    cq, cf, cv, cg = cols

    def spec(cb):
        return pl.BlockSpec((rows, w), lambda i: (i, cb))

    vec = pl.BlockSpec((1, w), lambda i: (0, 0))
    out_specs = [pl.BlockSpec((rows, w), lambda i: (i, 0))]
    out_shape = [jax.ShapeDtypeStruct((n, w), F32)]
    scratch = [pltpu.VMEM((HG_HEADS, HG_HEAD_DIM, batch * HG_HEAD_DIM), F32),
               pltpu.VMEM((HG_HEADS, 3, SUBLANES + HG_PAIR * HG_CHUNK * batch, HG_HEAD_DIM), F32)]
    fill_per_step = 0
    if zero_fill:
        fill_per_step, rem = divmod(zero_fill[0], ZERO_FILL_ROWS * (n // rows))
        assert rem == 0, "the fill is split evenly over the grid steps"
        out_specs.append(pl.BlockSpec(memory_space=pl.ANY))
        out_shape.append(jax.ShapeDtypeStruct(zero_fill, BF16))
        scratch += [pltpu.VMEM((ZERO_FILL_ROWS, zero_fill[1]), BF16), pltpu.SemaphoreType.DMA(())]
    outs = pl.pallas_call(
        functools.partial(_hgrn2_kernel, batch=batch, fill_per_step=fill_per_step),
        grid=(n // rows,),
        in_specs=[spec(cq), spec(cf), spec(cv), spec(cg), vec, vec],
        out_specs=out_specs,
        out_shape=out_shape,
        scratch_shapes=scratch,
        compiler_params=_cparams(("arbitrary",)),
        name="hgrn2",
    )(proj, proj, proj, proj, lb.reshape(1, w), norm_w.reshape(1, w))
    return outs if zero_fill else outs[0]


S5_TILE = 128
S5_MM_ROWS = 512


def _s5_kernel(u_ref, pin_ref, bm_ref, ar_ref, ai_ref, cm_ref, d_ref, gw_ref, gb_ref,
               o_ref, u2_ref, hs_ref, h_ref, ys_ref, *, batch):
    rows, w = u_ref.shape
    nblk = w // LANES
    groups = hs_ref.shape[0]
    gpb = groups // nblk
    tile2 = 2 * S5_TILE
    pair = 2 * batch
    steps = rows // batch

    @pl.when(pl.program_id(0) == 0)
    def _():
        h_ref[...] = jnp.zeros_like(h_ref)

    for t in range(rows // S5_TILE):
        ub = u_ref[t * S5_TILE:(t + 1) * S5_TILE, :].astype(BF16)
        up = _dot(pin_ref[...], ub).astype(BF16)
        u2_ref[t * tile2:(t + 1) * tile2, 0:w] = up[0:tile2]
        u2_ref[t * tile2:(t + 1) * tile2, w:2 * w] = up[tile2:2 * tile2]
    mm = min(S5_MM_ROWS, 2 * rows)
    for j in range(nblk):
        for c in range(2 * rows // mm):
            rs = slice(c * mm, (c + 1) * mm)
            lhs = jnp.concatenate([u2_ref[rs, j * LANES:(j + 1) * LANES],
                                   u2_ref[rs, w + j * LANES:w + (j + 1) * LANES]], axis=1)
            bu = _dot(lhs, bm_ref[j])
            for q in range(gpb):
                hs_ref[j * gpb + q, rs, :] = bu[:, q * LANES:(q + 1) * LANES]

    top = lax.broadcasted_iota(jnp.int32, (pair, LANES), 0) < batch
    per_pass = 8
    unroll = 4
    for p in range(groups // per_pass):
        gs = [p * per_pass + g for g in range(per_pass)]
        ar8 = [jnp.broadcast_to(ar_ref[:, g * LANES:(g + 1) * LANES], (pair, LANES)) for g in gs]
        ai8 = [jnp.where(top, -1.0, 1.0) * ai_ref[:, g * LANES:(g + 1) * LANES] for g in gs]

        def body(i, hs):
            hs = list(hs)
            for s in range(unroll):
                r0 = pl.multiple_of((i * unroll + s) * pair, pair)
                for k, g in enumerate(gs):
                    h = ar8[k] * hs[k] + ai8[k] * pltpu.roll(hs[k], batch, 0) + hs_ref[g, pl.ds(r0, pair), :]
                    hs_ref[g, pl.ds(r0, pair), :] = h
                    hs[k] = h
            return tuple(hs)

        h0 = tuple(h_ref[:, g * LANES:(g + 1) * LANES] for g in gs)
        hn = lax.fori_loop(0, steps // unroll, body, h0)
        for k, g in enumerate(gs):
            h_ref[:, g * LANES:(g + 1) * LANES] = hn[k]

    for b in range(batch):
        for j in range(nblk):
            parts = [hs_ref[j * gpb + q, pl.ds(off + b, steps, stride=pair), :]
                     for off in (0, batch) for q in range(gpb)]
            lhs = jnp.concatenate(parts, axis=1).astype(BF16)
            ys_ref[j, pl.ds(b, steps, stride=batch), :] = _dot(lhs, cm_ref[j])
    y = jnp.concatenate([ys_ref[j] for j in range(nblk)], axis=1) + d_ref[...] * u_ref[...].astype(F32)
    y = _gelu_tanh(y)
    z = _dot(y.astype(BF16), gw_ref[...].astype(BF16)) + gb_ref[...]
    o_ref[...] = y * _sigmoid(z)


def _s5_layout_matrix(batch):
    assert 2 * batch == SUBLANES, "one time step must fill the 8 sublanes"
    tile2 = 2 * S5_TILE
    r2 = np.arange(tile2)
    src = (r2 // (2 * batch)) * batch + r2 % batch
    is_re = (r2 % (2 * batch)) < batch
    onehot = (src[:, None] == np.arange(S5_TILE)[None, :])
    pin = np.concatenate([onehot & is_re[:, None], onehot & ~is_re[:, None]], axis=0)
    return jnp.asarray(pin.astype(np.float32), BF16)


def _s5(proj, col, mats, d, glu_w, glu_b, batch, rows):
    n = proj.shape[0]
    w = d.shape[0]
    bmat, ar, ai, cmat = mats
    nblk, _, cpb = bmat.shape
    nc = nblk * cpb
    pin = _s5_layout_matrix(batch)
    full = lambda a: pl.BlockSpec(a.shape, lambda i, nd=a.ndim: (0,) * nd)
    return pl.pallas_call(
        functools.partial(_s5_kernel, batch=batch),
        grid=(n // rows,),
        in_specs=[pl.BlockSpec((rows, w), lambda i: (i, col)),
                  full(pin), full(bmat), full(ar), full(ai), full(cmat),
                  pl.BlockSpec((1, w), lambda i: (0, 0)),
                  pl.BlockSpec((w, w), lambda i: (0, 0)),
                  pl.BlockSpec((1, w), lambda i: (0, 0))],
        out_specs=pl.BlockSpec((rows, w), lambda i: (i, 0)),
        out_shape=jax.ShapeDtypeStruct((n, w), F32),
        scratch_shapes=[pltpu.VMEM((2 * rows, 2 * w), BF16),
                        pltpu.VMEM((nc // LANES, 2 * rows, LANES), F32),
                        pltpu.VMEM((2 * batch, nc), F32),
                        pltpu.VMEM((w // LANES, rows, LANES), F32)],
        compiler_params=_cparams(("arbitrary",)),
        name="s5",
    )(proj, pin, bmat, ar, ai, cmat, d.reshape(1, w), glu_w, glu_b.reshape(1, w))


def _s5_matrices(a_re, a_im, b_re, b_im, c_re, c_im, log_dt):
    g, p = a_re.shape
    ch = b_re.shape[-1]
    gpb = LANES // ch
    nblk = g // gpb
    dt = jnp.exp(log_dt.astype(F32))[:, None]
    mag = jnp.exp(a_re * dt)
    abr = mag * jnp.cos(a_im * dt)
    abi = mag * jnp.sin(a_im * dt)
    den = a_re * a_re + a_im * a_im
    cr = ((abr - 1.0) * a_re + abi * a_im) / den
    ci = (abi * a_re - (abr - 1.0) * a_im) / den
    bbr = cr[..., None] * b_re - ci[..., None] * b_im
    bbi = cr[..., None] * b_im + ci[..., None] * b_re
    eye = jnp.eye(gpb, dtype=F32)

    def in_block(m):
        m = m.reshape(nblk, gpb, p, ch)
        return jnp.einsum('jgpc,gh->jgchp', m, eye).reshape(nblk, gpb * ch, gpb * p)

    def out_block(m):
        m = m.reshape(nblk, gpb, ch, p)
        return jnp.einsum('jgcp,gh->jgphc', m, eye).reshape(nblk, gpb * p, gpb * ch)

    bmat = jnp.concatenate([in_block(bbr), in_block(bbi)], axis=1).astype(BF16)
    cmat = jnp.concatenate([out_block(c_re), -out_block(c_im)], axis=1).astype(BF16)
    return bmat, abr.reshape(1, g * p), abi.reshape(1, g * p), cmat


def _rglru_kernel(xg_ref, xr_ref, cw_ref, cb_ref, wa_ref, ba_ref, wx_ref, bx_ref, sp_ref,
                  o_ref, xp_ref, a_ref, b_ref, h_ref, *, batch):
    rows, w = xr_ref.shape
    halo = SUBLANES * ((RG_CONV - 1) * batch // SUBLANES + 1)
    pair = 2 * batch

    @pl.when(pl.program_id(0) == 0)
    def _():
        xp_ref[0:halo, :] = jnp.zeros((halo, w), F32)
        h_ref[...] = jnp.zeros_like(h_ref)

    xr = xr_ref[...].astype(F32)
    xp_ref[halo:halo + rows, :] = xr
    xc = cb_ref[...] + cw_ref[RG_CONV - 1:RG_CONV, :] * xr
    for i in range(RG_CONV - 1):
        back = (RG_CONV - 1 - i) * batch
        xc = xc + cw_ref[i:i + 1, :] * xp_ref[halo - back:halo - back + rows, :]
    xp_ref[0:halo, :] = xr[rows - halo:rows, :]

    xcb = xc.astype(BF16)
    nb = w // RG_BLOCK_W
    ra, ri = [], []
    for hblk in range(nb):
        xs = xcb[:, hblk * RG_BLOCK_W:(hblk + 1) * RG_BLOCK_W]
        ra.append(_dot(xs, wa_ref[hblk].astype(BF16)))
        ri.append(_dot(xs, wx_ref[hblk].astype(BF16)))
    r = _sigmoid(jnp.concatenate(ra, axis=1) + ba_ref[...])
    ig = _sigmoid(jnp.concatenate(ri, axis=1) + bx_ref[...])
    log_a = (-RG_C) * r * sp_ref[...]
    a = jnp.exp(log_a)
    a_ref[...] = a
    b_ref[...] = jnp.sqrt(-jnp.tanh(log_a) * (a * a + 1.0)) * (ig * xc)

    top = lax.broadcasted_iota(jnp.int32, (pair, w), 0) < batch
    unroll = 4

    def body(i, h):
        for s in range(unroll):
            r0 = pl.multiple_of((i * unroll + s) * pair, pair)
            at = a_ref[pl.ds(r0, pair), :]
            bt = b_ref[pl.ds(r0, pair), :]
            n1 = at * h + bt
            h1 = jnp.where(top, n1, pltpu.roll(n1, batch, 0))
            n2 = at * h1 + bt
            b_ref[pl.ds(r0, pair), :] = jnp.where(top, n1, n2)
            h = jnp.where(top, pltpu.roll(n2, batch, 0), n2)
        return h

    h_ref[...] = lax.fori_loop(0, rows // (pair * unroll), body, h_ref[...])
    o_ref[...] = b_ref[...] * _gelu_tanh(xg_ref[...].astype(F32))


def _rglru(proj, cols, conv_w, conv_b, wa, ba, wx, bx, softplus_neg_lam, batch, rows):
    assert 2 * batch == SUBLANES, "two time steps must fill the 8 sublanes"
    n = proj.shape[0]
    w = conv_b.shape[0]
    cg, cx = cols
    nb = wa.shape[0]
    halo = SUBLANES * ((RG_CONV - 1) * batch // SUBLANES + 1)
    vec = pl.BlockSpec((1, w), lambda i: (0, 0))
    blk = pl.BlockSpec((nb, RG_BLOCK_W, RG_BLOCK_W), lambda i: (0, 0, 0))
    return pl.pallas_call(
        functools.partial(_rglru_kernel, batch=batch),
        grid=(n // rows,),
        in_specs=[pl.BlockSpec((rows, w), lambda i: (i, cg)),
                  pl.BlockSpec((rows, w), lambda i: (i, cx)),
                  pl.BlockSpec((RG_CONV, w), lambda i: (0, 0)),
                  vec, blk, vec, blk, vec, vec],
        out_specs=pl.BlockSpec((rows, w), lambda i: (i, 0)),
        out_shape=jax.ShapeDtypeStruct((n, w), F32),
        scratch_shapes=[pltpu.VMEM((halo + rows, w), F32),
                        pltpu.VMEM((rows, w), F32),
                        pltpu.VMEM((rows, w), F32),
                        pltpu.VMEM((2 * batch, w), F32)],
        compiler_params=_cparams(("arbitrary",)),
        name="rglru",
    )(proj, proj, conv_w, conv_b.reshape(1, w), wa, ba.reshape(1, w), wx, bx.reshape(1, w),
      softplus_neg_lam.reshape(1, w))


def _merge_kernel(x_ref, ya_ref, yb_ref, yc_ref, *rest, alpha, halves):
    n_gate = 3 * halves
    gate_refs = rest[:n_gate]
    wb_ref, bg_ref, wo_ref, lg_ref, lb_ref, o_ref = rest[n_gate:]
    m = None
    for kbr, y_ref in enumerate((ya_ref, yb_ref, yc_ref)):
        br = _dot(y_ref[...].astype(BF16), wb_ref[0, kbr].astype(BF16))
        gp = jnp.concatenate([gate_refs[kbr * halves + i][...] for i in range(halves)], axis=1)
        t = _sigmoid(gp.astype(F32) + bg_ref[kbr:kbr + 1, :]) * br
        m = t if m is None else m + t
    mix = _dot(m.astype(BF16), wo_ref[0].astype(BF16))
    o_ref[...] = _layernorm(alpha * x_ref[...] + mix, lg_ref[...], lb_ref[...])


def _merge(x, ya, yb, yc, proj, gate_col0, w_branch, b_gate, w_out, ln_g, ln_b, layer, alpha, tm):
    n, d = x.shape
    w = ya.shape[1]
    nbr = w_branch.shape[1]
    halves = d // w
    row = lambda width: pl.BlockSpec((tm, width), lambda i: (i, 0))
    gates = [pl.BlockSpec((tm, w), lambda i, c=gate_col0 + c: (i, c)) for c in range(nbr * halves)]
    vec = pl.BlockSpec((1, d), lambda i: (0, 0))
    return pl.pallas_call(
        functools.partial(_merge_kernel, alpha=alpha, halves=halves),
        grid=(n // tm,),
        in_specs=[row(d), row(w), row(w), row(w)] + gates + [
            pl.BlockSpec((1, nbr, w, d), lambda i: (layer, 0, 0, 0)),
            pl.BlockSpec((nbr, d), lambda i: (0, 0)),
            pl.BlockSpec((1, d, d), lambda i: (layer, 0, 0)), vec, vec],
        out_specs=row(d),
        out_shape=jax.ShapeDtypeStruct((n, d), F32),
        compiler_params=_cparams(("parallel",)),
        name="merge",
    )(x, ya, yb, yc, *([proj] * (nbr * halves)), w_branch, b_gate.reshape(nbr, d), w_out,
      ln_g.reshape(1, d), ln_b.reshape(1, d))


def _first_max_mask(cur, idx, axis):
    m = jnp.max(cur, axis=axis, keepdims=True)
    first = jnp.min(jnp.where(cur == m, idx, jnp.int32(2 ** 30)), axis=axis, keepdims=True)
    return idx == first


def _router_kernel(x_ref, wr_ref, rb_ref, pos_ref, w_ref, seg_ref):
    tm = x_ref.shape[0]
    e = N_EXPERTS
    per = e // N_EXPERT_GROUPS
    x = x_ref[...]
    wr = wr_ref[...]
    xh = x.astype(BF16)
    xl = (x - xh.astype(F32)).astype(BF16)
    wh = wr.astype(BF16)
    wl = (wr - wh.astype(F32)).astype(BF16)
    nt = (((1,), (1,)), ((), ()))
    logits = (lax.dot_general(wh, xh, nt, preferred_element_type=F32)
              + lax.dot_general(wh, xl, nt, preferred_element_type=F32)
              + lax.dot_general(wl, xh, nt, preferred_element_type=F32))
    scores = _sigmoid(logits)
    biased = scores + rb_ref[...]
    neg = jnp.float32(-jnp.inf)

    b3 = biased.reshape(N_EXPERT_GROUPS, per, tm)
    i3 = lax.broadcasted_iota(jnp.int32, b3.shape, 1)
    top1 = _first_max_mask(b3, i3, 1)
    m1 = jnp.max(b3, axis=1, keepdims=True)
    m2 = jnp.max(jnp.where(top1, neg, b3), axis=1, keepdims=True)
    gscore = (m1 + m2).reshape(N_EXPERT_GROUPS, tm)

    ig = lax.broadcasted_iota(jnp.int32, gscore.shape, 0)
    gsel = jnp.zeros(gscore.shape, F32)
    cur = gscore
    for _ in range(TOPK_GROUPS):
        pick = _first_max_mask(cur, ig, 0)
        gsel = jnp.where(pick, 1.0, gsel)
        cur = jnp.where(pick, neg, cur)

    gsel3 = jnp.broadcast_to(gsel.reshape(N_EXPERT_GROUPS, 1, tm), b3.shape)
    masked = jnp.where(gsel3 > 0.0, b3, neg).reshape(e, tm)
    ie = lax.broadcasted_iota(jnp.int32, masked.shape, 0)
    chosen = jnp.zeros(masked.shape, F32)
    cur = masked
    for _ in range(TOP_K):
        pick = _first_max_mask(cur, ie, 0)
        chosen = jnp.where(pick, 1.0, chosen)
        cur = jnp.where(pick, neg, cur)

    wsel = jnp.where(chosen > 0.0, scores, 0.0)
    wsel = wsel / jnp.sum(wsel, axis=0, keepdims=True) * ROUTED_SCALE
    w_ref[0] = wsel

    cb = chosen.astype(BF16)
    tok_r = lax.broadcasted_iota(jnp.int32, (tm, tm), 0)
    tok_c = lax.broadcasted_iota(jnp.int32, (tm, tm), 1)
    before = jnp.where(tok_r < tok_c, 1.0, 0.0).astype(BF16)
    prefix = _dot(cb, before)
    cnt = jnp.sum(chosen, axis=1, keepdims=True)
    seg = jnp.floor((cnt + (SUB_ROWS - 1)) * (1.0 / SUB_ROWS))
    seg_b = jnp.broadcast_to(seg, (e, LANES))
    ex_r = lax.broadcasted_iota(jnp.int32, (e, e), 0)
    ex_c = lax.broadcasted_iota(jnp.int32, (e, e), 1)
    earlier = jnp.where(ex_c < ex_r, 1.0, 0.0).astype(BF16)
    seg_off = _dot(earlier, seg_b.astype(BF16))
    seg_ref[0] = seg_b
    pos_ref[0] = jnp.where(chosen > 0.0, seg_off[:, 0:1] * SUB_ROWS + prefix, -1.0)


def _router(x, router_w_t, router_bias, tm):
    n, d = x.shape
    e = router_w_t.shape[0]
    nt = n // tm
    tile = lambda width: pl.BlockSpec((1, e, width), lambda i: (i, 0, 0))
    return pl.pallas_call(
        _router_kernel,
        grid=(nt,),
        in_specs=[pl.BlockSpec((tm, d), lambda i: (i, 0)),
                  pl.BlockSpec((e, d), lambda i: (0, 0)),
                  pl.BlockSpec((e, 1), lambda i: (0, 0))],
        out_specs=[tile(tm), tile(tm), tile(LANES)],
        out_shape=[jax.ShapeDtypeStruct((nt, e, tm), F32),
                   jax.ShapeDtypeStruct((nt, e, tm), F32),
                   jax.ShapeDtypeStruct((nt, e, LANES), F32)],
        compiler_params=_cparams(("parallel",)),
        name="router",
    )(x, router_w_t, router_bias.reshape(e, 1))


def _tile_buffer_rows(tile, n_experts, top_k):
    rows = tile * top_k + n_experts * (SUB_ROWS - 1)
    return -(-rows // MM_ROWS) * MM_ROWS


def _slot_matrix(j, e, pos_ref, val_ref, tile):
    e = jnp.maximum(e, 0)
    prow = pos_ref[0, pl.ds(e, 1), :] - jnp.asarray(j * SUB_ROWS).astype(F32)
    tgt = lax.broadcasted_iota(jnp.int32, (SUB_ROWS, tile), 0).astype(F32)
    if val_ref is None:
        return jnp.where(prow == tgt, 1.0, 0.0).astype(BF16)
    return jnp.where(prow == tgt, val_ref[0, pl.ds(e, 1), :], 0.0).astype(BF16)


LOOP_UNROLL = 8


def _wait_sub_blocks(src, dst, sem, n_sub):
    def wait(i, c):
        for _ in range(LOOP_UNROLL):
            pltpu.make_async_copy(src.at[pl.ds(0, SUB_ROWS)], dst.at[pl.ds(0, SUB_ROWS)], sem).wait()
        return c

    lax.fori_loop(0, n_sub // LOOP_UNROLL, wait, 0)


def _dispatch_kernel(sub_e_ref, sub_dst_ref, x_ref, pos_ref, init_ref, xs_hbm, m_ref, xs_ref, sem):
    del init_ref
    t = pl.program_id(0)
    tile = x_ref.shape[0]
    n_sub = m_ref.shape[0] // SUB_ROWS
    base = t * n_sub
    slot = t % 2

    def build(j, c):
        r0 = pl.multiple_of(j * SUB_ROWS, SUB_ROWS)
        m_ref[pl.ds(r0, SUB_ROWS), :] = _slot_matrix(j, sub_e_ref[base + j], pos_ref, None, tile)
        return c

    lax.fori_loop(0, n_sub, build, 0, unroll=LOOP_UNROLL)
    xb = x_ref[...].astype(BF16)
    for c in range(m_ref.shape[0] // MM_ROWS):
        rs = slice(c * MM_ROWS, (c + 1) * MM_ROWS)
        xs_ref[slot, rs, :] = _dot(m_ref[rs, :], xb).astype(BF16)

    def start(j, c):
        r0 = pl.multiple_of(j * SUB_ROWS, SUB_ROWS)
        d0 = pl.multiple_of(sub_dst_ref[base + j] * SUB_ROWS, SUB_ROWS)
        pltpu.make_async_copy(xs_ref.at[slot, pl.ds(r0, SUB_ROWS)], xs_hbm.at[pl.ds(d0, SUB_ROWS)],
                              sem.at[slot]).start()
        return c

    lax.fori_loop(0, n_sub, start, 0, unroll=LOOP_UNROLL)

    @pl.when(t > 0)
    def _():
        _wait_sub_blocks(xs_ref.at[1 - slot], xs_hbm, sem.at[1 - slot], n_sub)

    @pl.when(t == pl.num_programs(0) - 1)
    def _():
        _wait_sub_blocks(xs_ref.at[slot], xs_hbm, sem.at[slot], n_sub)


def _dispatch(x, pos, init, sub_e, sub_dst, buf_rows):
    nt, e, tile = pos.shape
    d = init.shape[1]
    return pl.pallas_call(
        _dispatch_kernel,
        grid_spec=pltpu.PrefetchScalarGridSpec(
            num_scalar_prefetch=2,
            grid=(nt,),
            in_specs=[pl.BlockSpec((tile, d), lambda i, *_: (i, 0)),
                      pl.BlockSpec((1, e, tile), lambda i, *_: (i, 0, 0)),
                      pl.BlockSpec(memory_space=pl.ANY)],
            out_specs=pl.BlockSpec(memory_space=pl.ANY),
            scratch_shapes=[pltpu.VMEM((buf_rows, tile), BF16),
                            pltpu.VMEM((2, buf_rows, d), BF16),
                            pltpu.SemaphoreType.DMA((2,))]),
        out_shape=jax.ShapeDtypeStruct(init.shape, BF16),
        input_output_aliases={4: 0},
        compiler_params=_cparams(("arbitrary",)),
        name="moe_dispatch",
    )(sub_e.reshape(-1), sub_dst.reshape(-1), x, pos, init)


def _experts_kernel(blk_e_ref, n_used_ref, xs_ref, w1_ref, w3_ref, w2_ref, ys_ref, w13_ref, w2b_ref):
    i = pl.program_id(0)
    hid = w1_ref.shape[-1]
    used = i < n_used_ref[0]
    new_expert = jnp.logical_or(i == 0, blk_e_ref[i] != blk_e_ref[jnp.maximum(i - 1, 0)])

    @pl.when(jnp.logical_and(used, new_expert))
    def _():
        w13_ref[:, 0:hid] = w1_ref[0, 0].astype(BF16)
        w13_ref[:, hid:2 * hid] = w3_ref[0, 0].astype(BF16)
        w2b_ref[...] = w2_ref[0, 0].astype(BF16)

    @pl.when(used)
    def _():
        h = _dot(xs_ref[...], w13_ref[...])
        hh = (_silu(h[:, 0:hid]) * h[:, hid:2 * hid]).astype(BF16)
        ys_ref[...] = _dot(hh, w2b_ref[...]).astype(ys_ref.dtype)


def _experts(xs, blk_e, n_used, w1, w3, w2, layer):
    rows, d = xs.shape
    hid = w1.shape[-1]
    nblk = rows // FFN_ROWS
    rowblk = pl.BlockSpec((FFN_ROWS, d), lambda i, be, nu: (jnp.minimum(i, nu[0] - 1), 0))
    return pl.pallas_call(
        _experts_kernel,
        grid_spec=pltpu.PrefetchScalarGridSpec(
            num_scalar_prefetch=2,
            grid=(nblk,),
            in_specs=[rowblk,
                      pl.BlockSpec((1, 1, d, hid), lambda i, be, nu: (layer, be[i], 0, 0)),
                      pl.BlockSpec((1, 1, d, hid), lambda i, be, nu: (layer, be[i], 0, 0)),
                      pl.BlockSpec((1, 1, hid, d), lambda i, be, nu: (layer, be[i], 0, 0))],
            out_specs=rowblk,
            scratch_shapes=[pltpu.VMEM((d, 2 * hid), BF16), pltpu.VMEM((hid, d), BF16)]),
        out_shape=jax.ShapeDtypeStruct((rows, d), BF16),
        input_output_aliases={2: 0},
        compiler_params=_cparams(("arbitrary",)),
        name="moe_experts",
    )(blk_e, n_used, xs, w1, w3, w2)


def _combine_kernel(sub_e_ref, sub_dst_ref, x_ref, pos_ref, w_ref, ys_hbm, s1_ref, s3_ref, s2_ref,
                    lg_ref, lb_ref, o_ref, m_ref, ys_ref, sem, *maybe_slab, alpha):
    t = pl.program_id(0)
    tile = x_ref.shape[0]
    n_sub = m_ref.shape[0] // SUB_ROWS
    base = t * n_sub
    slot = t % 2
    last = t == pl.num_programs(0) - 1

    def start(j, tile_base, sl):
        r0 = pl.multiple_of(j * SUB_ROWS, SUB_ROWS)
        d0 = pl.multiple_of(sub_dst_ref[tile_base + j] * SUB_ROWS, SUB_ROWS)
        pltpu.make_async_copy(ys_hbm.at[pl.ds(d0, SUB_ROWS)], ys_ref.at[sl, pl.ds(r0, SUB_ROWS)],
                              sem.at[sl]).start()

    def build(j, c):
        r0 = pl.multiple_of(j * SUB_ROWS, SUB_ROWS)
        m_ref[pl.ds(r0, SUB_ROWS), :] = _slot_matrix(j, sub_e_ref[base + j], pos_ref, w_ref, tile)
        return c

    def start_own(j, c):
        start(j, base, slot)
        return c

    def start_next(j, c):
        start(j, base + n_sub, 1 - slot)
        return c

    @pl.when(t == 0)
    def _():
        lax.fori_loop(0, n_sub, start_own, 0, unroll=LOOP_UNROLL)

    @pl.when(jnp.logical_not(last))
    def _():
        lax.fori_loop(0, n_sub, start_next, 0, unroll=LOOP_UNROLL)

    lax.fori_loop(0, n_sub, build, 0, unroll=LOOP_UNROLL)
    _wait_sub_blocks(ys_hbm, ys_ref.at[slot], sem.at[slot], n_sub)

    x = x_ref[...]
    xb = x.astype(BF16)
    g1 = _dot(xb, s1_ref[0].astype(BF16))
    g3 = _dot(xb, s3_ref[0].astype(BF16))
    ffn = _dot((_silu(g1) * g3).astype(BF16), s2_ref[0].astype(BF16))
    tn = (((0,), (0,)), ((), ()))
    for c in range(m_ref.shape[0] // MM_ROWS):
        rs = slice(c * MM_ROWS, (c + 1) * MM_ROWS)
        ffn = ffn + lax.dot_general(m_ref[rs, :], ys_ref[slot, rs, :], tn, preferred_element_type=F32)
    y = _layernorm(alpha * x + ffn, lg_ref[...], lb_ref[...])
    if maybe_slab:
        _split_batches(y, o_ref, maybe_slab[0])
    else:
        o_ref[...] = y


def _combine(x, pos, wsel, ys, sub_e, sub_dst, s1, s3, s2, ln_g, ln_b, layer, alpha, buf_rows, out_batch):
    n, d = x.shape
    nt, e, tile = pos.shape
    hid = s1.shape[-1]
    vec = pl.BlockSpec((1, d), lambda i, *_: (0, 0))
    scratch = [pltpu.VMEM((buf_rows, tile), BF16), pltpu.VMEM((2, buf_rows, d), BF16),
               pltpu.SemaphoreType.DMA((2,))]
    if out_batch:
        out = jax.ShapeDtypeStruct((out_batch, n // out_batch, d), F32)
        out_spec = pl.BlockSpec((out_batch, tile // out_batch, d), lambda i, *_: (0, i, 0))
        scratch.append(pltpu.VMEM((d // LANES, tile, LANES), F32))
    else:
        out = jax.ShapeDtypeStruct((n, d), F32)
        out_spec = pl.BlockSpec((tile, d), lambda i, *_: (i, 0))
    return pl.pallas_call(
        functools.partial(_combine_kernel, alpha=alpha),
        grid_spec=pltpu.PrefetchScalarGridSpec(
            num_scalar_prefetch=2,
            grid=(nt,),
            in_specs=[pl.BlockSpec((tile, d), lambda i, *_: (i, 0)),
                      pl.BlockSpec((1, e, tile), lambda i, *_: (i, 0, 0)),
                      pl.BlockSpec((1, e, tile), lambda i, *_: (i, 0, 0)),
                      pl.BlockSpec(memory_space=pl.ANY),
                      pl.BlockSpec((1, d, hid), lambda i, *_: (layer, 0, 0)),
                      pl.BlockSpec((1, d, hid), lambda i, *_: (layer, 0, 0)),
                      pl.BlockSpec((1, hid, d), lambda i, *_: (layer, 0, 0)), vec, vec],
            out_specs=out_spec,
            scratch_shapes=scratch),
        out_shape=out,
        compiler_params=_cparams(("arbitrary",)),
        name="moe_combine",
    )(sub_e.reshape(-1), sub_dst.reshape(-1), x, pos, wsel, ys, s1, s3, s2,
      ln_g.reshape(1, d), ln_b.reshape(1, d))


def _dispatch_plan(seg, buf_rows, n_sorted_rows, total_rows):
    nt, ne = seg.shape
    n_sub = buf_rows // SUB_ROWS
    per_blk = FFN_ROWS // SUB_ROWS
    seg = seg.astype(jnp.int32)
    seg_end = jnp.cumsum(seg, axis=1)
    seg_start = seg_end - seg
    exp_sub = jnp.sum(seg, axis=0)
    exp_blk = (exp_sub + per_blk - 1) // per_blk
    blk_end = jnp.cumsum(exp_blk)
    exp_start = (blk_end - exp_blk) * per_blk
    dst_start = exp_start[None, :] + jnp.cumsum(seg, axis=0) - seg
    j = jnp.arange(n_sub, dtype=jnp.int32)
    sub_e = jnp.sum((seg_end[:, None, :] <= j[None, :, None]).astype(jnp.int32), axis=-1)
    used = j[None, :] < seg_end[:, -1:]
    sub_ec = jnp.minimum(sub_e, ne - 1)
    pick = sub_ec[:, :, None] == jnp.arange(ne, dtype=jnp.int32)[None, None, :]
    dst = j[None, :] + jnp.sum(jnp.where(pick, (dst_start - seg_start)[:, None, :], 0), axis=-1)
    spare = n_sorted_rows // SUB_ROWS + (jnp.arange(nt, dtype=jnp.int32)[:, None] % 2) * n_sub + j[None, :]
    sub_dst = jnp.where(used, dst, spare).astype(jnp.int32)
    sub_e = jnp.where(used, sub_ec, -1).astype(jnp.int32)
    nblk = total_rows // FFN_ROWS
    i = jnp.arange(nblk, dtype=jnp.int32)
    blk_e = jnp.minimum(jnp.sum((blk_end[None, :] <= i[:, None]).astype(jnp.int32), axis=-1), ne - 1)
    return sub_e, sub_dst, blk_e, blk_end[-1:].astype(jnp.int32)


def kernel(x, w_in, b_gate, hgrn_lb_logits, hgrn_norm_w, s5_a_re, s5_a_im, s5_b_re, s5_b_im,
           s5_c_re, s5_c_im, s5_d, s5_log_dt, s5_glu_w, s5_glu_b, rg_conv_w, rg_conv_b,
           rg_wa, rg_ba, rg_wx, rg_bx, rg_lambda, w_branch, w_out, ln1_g, ln1_b,
           router_w, router_bias, exp_w1, exp_w3, exp_w2, sh_w1, sh_w3, sh_w2, ln2_g, ln2_b):
    bn, s, d = x.shape
    depth = w_in.shape[0]
    n = bn * s
    w = hgrn_norm_w.shape[1]
    alpha = (2 * depth) ** 0.25

    sm = jax.nn.softmax(hgrn_lb_logits.astype(F32), axis=0)
    lower_bounds = jnp.cumsum(sm, axis=0) - sm[0:1]

    xt = x
    sorted_init = None
    seq_rows = SEQ_STEPS * bn
    c_hg = (0, 1, 2, 3)
    c_su = 4
    c_rg = (5, 6)
    c_gate = 7
    n_exp = router_w.shape[2]
    buf_rows = _tile_buffer_rows(MOE_TILE, n_exp, TOP_K)
    per_blk = FFN_ROWS // SUB_ROWS
    max_sub = n * TOP_K // SUB_ROWS + (n // MOE_TILE) * n_exp + n_exp * (per_blk - 1)
    n_sorted = -(-max_sub // per_blk) * FFN_ROWS
    fill_unit = ZERO_FILL_ROWS * (n // seq_rows)
    sorted_rows = -(-(n_sorted + 2 * buf_rows) // fill_unit) * fill_unit
    for l in range(depth):
        proj, xt = _in_proj(xt, w_in, l, PROJ_ROWS, w)
        if sorted_init is None:
            ya, sorted_init = _hgrn2(proj, c_hg, lower_bounds[l], hgrn_norm_w[l], bn, seq_rows,
                                     zero_fill=(sorted_rows, d))
        else:
            ya = _hgrn2(proj, c_hg, lower_bounds[l], hgrn_norm_w[l], bn, seq_rows)
        mats = _s5_matrices(s5_a_re[l], s5_a_im[l], s5_b_re[l], s5_b_im[l],
                            s5_c_re[l], s5_c_im[l], s5_log_dt[l])
        yb = _s5(proj, c_su, mats, s5_d[l], s5_glu_w[l], s5_glu_b[l], bn, seq_rows)
        yc = _rglru(proj, c_rg, rg_conv_w[l], rg_conv_b[l], rg_wa[l], rg_ba[l], rg_wx[l], rg_bx[l],
                    jax.nn.softplus(-rg_lambda[l].astype(F32)), bn, seq_rows)
        x1 = _merge(xt, ya, yb, yc, proj, c_gate, w_branch, b_gate[l], w_out,
                    ln1_g[l], ln1_b[l], l, alpha, MERGE_ROWS)
        pos, wsel, seg = _router(x1, router_w[l].T, router_bias[l], MOE_TILE)
        sub_e, sub_dst, blk_e, n_used = _dispatch_plan(seg[:, :, 0], buf_rows, n_sorted, sorted_rows)
        xs = _dispatch(x1, pos, sorted_init, sub_e, sub_dst, buf_rows)
        ys = _experts(xs, blk_e, n_used, exp_w1, exp_w3, exp_w2, l)
        xt = _combine(x1, pos, wsel, ys, sub_e, sub_dst, sh_w1, sh_w3, sh_w2, ln2_g[l], ln2_b[l],
                      l, alpha, buf_rows, out_batch=bn if l == depth - 1 else 0)
        sorted_init = ys
    return xt
```
